```python
import math
import jax, jax.numpy as jnp
from jax import lax
import numpy as np

D_MODEL = 1024
BATCH = 2
SEQ = 8192
DEPTH = 4
DEC_BATCH = 32
DEC_SEQ = 1
PAST_LEN = 8192
PAGE_SIZE = 128

HEAD_DIM = 64
A_HEADS = 4
A_QK = A_HEADS * 2 * HEAD_DIM
A_VDIM = 2 * HEAD_DIM
A_V = A_HEADS * A_VDIM
ATTN_SCALE = HEAD_DIM ** -0.5
ROPE_THETA = 10000.0
Q_BLOCK = 128
B_WIDTH = D_MODEL // 2
CONV_W = 3
EVEN_IN = 2 * A_QK + A_V + 3 * B_WIDTH
EVEN_OUT = A_V + B_WIDTH
HG_HEADS = 8
HG_DK = 128
HG_FDIM = HG_HEADS * HG_DK
HG_DV = D_MODEL // HG_HEADS
ODD_IN = 2 * HG_FDIM + 2 * D_MODEL
GLA_CHUNK = 64
D_FF = -(-8 * D_MODEL // (3 * 256)) * 256
N_EVEN = (DEPTH + 1) // 2
N_ODD = DEPTH // 2
EPS = 1e-6

kernel_name = 'hybrid_diffattn_shortconv_hgrn2_decode_step'


def _rmsnorm(x, w):
    xf = x.astype(jnp.float32)
    y = xf * lax.rsqrt(jnp.mean(xf * xf, axis=-1, keepdims=True) + EPS)
    return (y * w.astype(jnp.float32)).astype(x.dtype)


def _rope(x, pos):
    half = x.shape[-1] // 2
    inv = ROPE_THETA ** (-jnp.arange(half, dtype=jnp.float32) / half)
    ang = pos.astype(jnp.float32)[:, None] * inv[None, :]
    cos = jnp.cos(ang)[:, None, None, :]
    sin = jnp.sin(ang)[:, None, None, :]
    xf = x.astype(jnp.float32)
    x1, x2 = xf[..., :half], xf[..., half:]
    return jnp.concatenate([x1 * cos - x2 * sin, x1 * sin + x2 * cos], axis=-1).astype(x.dtype)


def _diff_attend(q, k, v, mask, lam):
    s = jnp.einsum('bqhmd,bkhmd->bhmqk', q, k).astype(jnp.float32) * ATTN_SCALE
    p = jax.nn.softmax(jnp.where(mask, s, -jnp.inf), axis=-1)
    p = p[:, :, 0] - lam * p[:, :, 1]
    return jnp.einsum('bhqk,bkhe->bqhe', p.astype(v.dtype), v)


def _diff_attn_prompt(q, k, v, lam):
    B, T = q.shape[:2]
    nb = T // Q_BLOCK
    qb = q.reshape(B, nb, Q_BLOCK, A_HEADS, 2, HEAD_DIM).swapaxes(0, 1)
    kpos = jnp.arange(T)

    def blk(args):
        qi, start = args
        qpos = start + jnp.arange(Q_BLOCK)
        return _diff_attend(qi, k, v, kpos[None, :] <= qpos[:, None], lam)

    o = lax.map(blk, (qb, jnp.arange(nb) * Q_BLOCK))
    return o.swapaxes(0, 1).reshape(B, T, A_HEADS, A_VDIM)


def _even_mixer(xn, pos, layer_idx, w_in, w_out, conv_w, lq1, lk1, lq2, lk2, subln_w,
                past_k, past_v, conv_state):
    B, T, _ = xn.shape
    z = xn @ w_in
    q, k, v, bg, cg, xin = jnp.split(
        z, [A_QK, 2 * A_QK, 2 * A_QK + A_V, 2 * A_QK + A_V + B_WIDTH, 2 * A_QK + A_V + 2 * B_WIDTH], axis=-1)
    q = _rope(q.reshape(B, T, A_HEADS, 2, HEAD_DIM), pos)
    k = _rope(k.reshape(B, T, A_HEADS, 2, HEAD_DIM), pos)
    v = v.reshape(B, T, A_HEADS, A_VDIM)
    lam_init = 0.8 - 0.6 * math.exp(-0.3 * layer_idx)
    lam = (jnp.exp(jnp.sum(lq1.astype(jnp.float32) * lk1.astype(jnp.float32)))
           - jnp.exp(jnp.sum(lq2.astype(jnp.float32) * lk2.astype(jnp.float32))) + lam_init)
    if past_k is None:
        o = _diff_attn_prompt(q, k, v, lam)
        pad = jnp.zeros((B, CONV_W - 1, B_WIDTH), xn.dtype)
    else:
        P = past_k.shape[1]
        mask = jnp.concatenate([jnp.ones((T, P), bool), jnp.tril(jnp.ones((T, T), bool))], axis=1)
        k_all = jnp.concatenate([past_k, k.astype(past_k.dtype)], axis=1)
        v_all = jnp.concatenate([past_v, v.astype(past_v.dtype)], axis=1)
        o = _diff_attend(q, k_all, v_all, mask, lam)
        pad = conv_state
    o = _rmsnorm(o, subln_w) * (1.0 - lam_init)
    u = cg * xin
    ext = jnp.concatenate([pad.astype(u.dtype), u], axis=1)
    conv = ext[:, 0:T] * conv_w[0]
    for j in range(1, CONV_W):
        conv = conv + ext[:, j:j + T] * conv_w[j]
    yb = bg * conv
    y = jnp.concatenate([o.reshape(B, T, A_V).astype(yb.dtype), yb], axis=-1) @ w_out
    return y, k, v, ext[:, T:]


def _gla_chunked(q, k, v, log_f, S0):
    B, T, H, dk = q.shape
    dv = v.shape[-1]
    C = GLA_CHUNK if T % GLA_CHUNK == 0 else T
    n = T // C

    def chunks(a):
        return a.astype(jnp.float32).reshape(B, n, C, H, a.shape[-1]).transpose(1, 0, 3, 2, 4)

    causal = jnp.tril(jnp.ones((C, C), bool))

    def step(S, inp):
        qc, kc, vc, gc = inp
        G = jnp.cumsum(gc, axis=2)
        o_inter = jnp.einsum('bhtk,bhkv->bhtv', qc * jnp.exp(G), S)
        diff = G[:, :, :, None, :] - G[:, :, None, :, :]
        decay = jnp.exp(jnp.where(causal[:, :, None], diff, -jnp.inf))
        A = jnp.einsum('bhtk,bhsk,bhtsk->bhts', qc, kc, decay)
        o_intra = jnp.einsum('bhts,bhsv->bhtv', A, vc)
        G_last = G[:, :, -1:, :]
        S_new = (jnp.exp(G_last)[:, :, 0, :, None] * S
                 + jnp.einsum('bhsk,bhsv->bhkv', kc * jnp.exp(G_last - G), vc))
        return S_new, o_inter + o_intra

    S, o = lax.scan(step, S0.astype(jnp.float32), (chunks(q), chunks(k), chunks(v), chunks(log_f)))
    return o.transpose(1, 0, 3, 2, 4).reshape(B, T, H, dv), S


def _hgrn2_mixer(xn, lb, w_in, w_out, gnorm_w, state0):
    B, T, _ = xn.shape
    z = xn @ w_in
    q, f, i, g = jnp.split(z, [HG_FDIM, 2 * HG_FDIM, 2 * HG_FDIM + D_MODEL], axis=-1)
    fg = lb + (1.0 - lb) * jax.nn.sigmoid(f.astype(jnp.float32))
    log_f = jnp.log(fg)
    kk = 1.0 - fg

    def heads(a, d):
        return a.reshape(B, T, HG_HEADS, d)

    o, S = _gla_chunked(heads(jax.nn.silu(q.astype(jnp.float32)), HG_DK), heads(kk, HG_DK),
                        heads(i, HG_DV), heads(log_f, HG_DK), state0)
    o = _rmsnorm(o, gnorm_w) * jax.nn.silu(heads(g.astype(jnp.float32), HG_DV))
    y = o.reshape(B, T, D_MODEL).astype(xn.dtype) @ w_out
    return y, S.astype(state0.dtype)


def _swiglu(x, w_in, w_out):
    gate, up = jnp.split(x @ w_in, 2, axis=-1)
    return (jax.nn.silu(gate) * up) @ w_out


def setup_inputs(seed: int = 0) -> dict:
    key = jax.random.key(seed)
    ks = jax.random.split(key, 22)
    n_pages = PAST_LEN // PAGE_SIZE
    n_phys = (5 * DEC_BATCH * n_pages) // 4

    def nrm(k, shape, scale):
        return scale * jax.random.normal(k, shape, jnp.float32)

    page_table = jax.random.permutation(ks[6], n_phys)[:DEC_BATCH * n_pages]
    page_table = page_table.reshape(DEC_BATCH, n_pages).astype(jnp.int32)
    return {
        'x_prompt': nrm(ks[0], (BATCH, SEQ, D_MODEL), 1.0),
        'x_sample': nrm(ks[1], (DEC_BATCH, DEC_SEQ, D_MODEL), 1.0),
        'cache_k': nrm(ks[2], (N_EVEN, n_phys, PAGE_SIZE, A_HEADS, 2, HEAD_DIM), 1.0),
        'cache_v': nrm(ks[3], (N_EVEN, n_phys, PAGE_SIZE, A_HEADS, A_VDIM), 1.0),
        'state_conv': nrm(ks[4], (N_EVEN, DEC_BATCH, CONV_W - 1, B_WIDTH), 1.0),
        'state_hgrn': nrm(ks[5], (N_ODD, DEC_BATCH, HG_HEADS, HG_DK, HG_DV), 0.5),
        'page_table': page_table,
        'norm_w': 1.0 + nrm(ks[7], (DEPTH, 4, D_MODEL), 0.02),
        'w_in_even': nrm(ks[8], (N_EVEN, D_MODEL, EVEN_IN), D_MODEL ** -0.5),
        'w_out_even': nrm(ks[9], (N_EVEN, EVEN_OUT, D_MODEL), EVEN_OUT ** -0.5),
        'conv_w': nrm(ks[10], (N_EVEN, CONV_W, B_WIDTH), CONV_W ** -0.5),
        'lambda_q1': nrm(ks[11], (N_EVEN, HEAD_DIM), 0.1),
        'lambda_k1': nrm(ks[12], (N_EVEN, HEAD_DIM), 0.1),
        'lambda_q2': nrm(ks[13], (N_EVEN, HEAD_DIM), 0.1),
        'lambda_k2': nrm(ks[14], (N_EVEN, HEAD_DIM), 0.1),
        'subln_w': 1.0 + nrm(ks[15], (N_EVEN, A_VDIM), 0.02),
        'w_in_odd': nrm(ks[16], (N_ODD, D_MODEL, ODD_IN), D_MODEL ** -0.5),
        'w_out_odd': nrm(ks[17], (N_ODD, D_MODEL, D_MODEL), D_MODEL ** -0.5),
        'hgrn_lb': nrm(ks[18], (DEPTH, HG_FDIM), 0.1),
        'gnorm_w': 1.0 + nrm(ks[19], (N_ODD, HG_DV), 0.02),
        'w_ffn_in': nrm(ks[20], (DEPTH, D_MODEL, 2 * D_FF), D_MODEL ** -0.5),
        'w_ffn_out': nrm(ks[21], (DEPTH, D_FF, D_MODEL), D_FF ** -0.5),
    }


def reference(x_prompt, x_sample, cache_k, cache_v, state_conv, state_hgrn, page_table,
              norm_w, w_in_even, w_out_even, conv_w, lambda_q1, lambda_k1, lambda_q2, lambda_k2,
              subln_w, w_in_odd, w_out_odd, hgrn_lb, gnorm_w, w_ffn_in, w_ffn_out):
    n_dec, t_dec = x_sample.shape[:2]
    past_len = page_table.shape[1] * PAGE_SIZE
    pos_p = jnp.arange(x_prompt.shape[1])
    pos_s = past_len + jnp.arange(t_dec)
    p_lb = jax.nn.softmax(hgrn_lb.astype(jnp.float32), axis=0)
    lower_bounds = jnp.cumsum(p_lb, axis=0) - p_lb[0]
    xp, xs = x_prompt, x_sample
    kp_l, vp_l, cp_l, hp_l = [], [], [], []
    ks_l, vs_l, cs_l, hs_l = [], [], [], []
    for layer in range(DEPTH):
        if layer % 2 == 0:
            e = layer // 2
            wts = (w_in_even[e], w_out_even[e], conv_w[e], lambda_q1[e], lambda_k1[e],
                   lambda_q2[e], lambda_k2[e], subln_w[e])
            yp, kp, vp, cp = _even_mixer(_rmsnorm(xp, norm_w[layer, 0]), pos_p, layer, *wts,
                                         None, None, None)
            past_k = cache_k[e, page_table].reshape(n_dec, past_len, A_HEADS, 2, HEAD_DIM)
            past_v = cache_v[e, page_table].reshape(n_dec, past_len, A_HEADS, A_VDIM)
            ys, ksm, vsm, csm = _even_mixer(_rmsnorm(xs, norm_w[layer, 0]), pos_s, layer, *wts,
                                            past_k, past_v, state_conv[e])
            kp_l.append(kp); vp_l.append(vp); cp_l.append(cp)
            ks_l.append(ksm); vs_l.append(vsm); cs_l.append(csm)
        else:
            o = layer // 2
            zeros = jnp.zeros((xp.shape[0], HG_HEADS, HG_DK, HG_DV), jnp.float32)
            yp, sp = _hgrn2_mixer(_rmsnorm(xp, norm_w[layer, 0]), lower_bounds[layer],
                                  w_in_odd[o], w_out_odd[o], gnorm_w[o], zeros)
            ys, ss = _hgrn2_mixer(_rmsnorm(xs, norm_w[layer, 0]), lower_bounds[layer],
                                  w_in_odd[o], w_out_odd[o], gnorm_w[o], state_hgrn[o])
            hp_l.append(sp); hs_l.append(ss)
        xp = xp + _rmsnorm(yp, norm_w[layer, 1])
        xs = xs + _rmsnorm(ys, norm_w[layer, 1])
        xp = xp + _rmsnorm(_swiglu(_rmsnorm(xp, norm_w[layer, 2]), w_ffn_in[layer], w_ffn_out[layer]),
                           norm_w[layer, 3])
        xs = xs + _rmsnorm(_swiglu(_rmsnorm(xs, norm_w[layer, 2]), w_ffn_in[layer], w_ffn_out[layer]),
                           norm_w[layer, 3])
    return (xp, xs, jnp.stack(kp_l), jnp.stack(vp_l), jnp.stack(cp_l), jnp.stack(hp_l),
            jnp.stack(ks_l), jnp.stack(vs_l), jnp.stack(cs_l), jnp.stack(hs_l))
```

```python
import functools
import math

import jax
import jax.numpy as jnp
from jax import lax
from jax.experimental import pallas as pl
from jax.experimental.pallas import tpu as pltpu

D_MODEL = 1024
PAGE_SIZE = 128
HEAD_DIM = 64
A_HEADS = 4
A_QK = A_HEADS * 2 * HEAD_DIM
A_VDIM = 2 * HEAD_DIM
A_V = A_HEADS * A_VDIM
ATTN_SCALE = HEAD_DIM ** -0.5
ROPE_THETA = 10000.0
B_WIDTH = D_MODEL // 2
CONV_W = 3
EVEN_IN = 2 * A_QK + A_V + 3 * B_WIDTH
HG_HEADS = 8
HG_DK = 128
HG_FDIM = HG_HEADS * HG_DK
HG_DV = D_MODEL // HG_HEADS
ODD_IN = 2 * HG_FDIM + 2 * D_MODEL
D_FF = -(-8 * D_MODEL // (3 * 256)) * 256
EPS = 1e-6

LANES = 128
SUBLANES = 8
VMEM_LIMIT_BYTES = 56 * 1024 * 1024
GLA_SUB = 16
FFN_CHUNK = 256

F32 = jnp.float32
BF16 = jnp.bfloat16


def _cparams(semantics):
    return pltpu.CompilerParams(dimension_semantics=semantics,
                                vmem_limit_bytes=VMEM_LIMIT_BYTES)


def _rms(x, w):
    return x * lax.rsqrt(jnp.mean(x * x, axis=-1, keepdims=True) + EPS) * w


def _sigmoid(x):
    return 1.0 / (1.0 + jnp.exp(-x))


def _const_spec(shape):
    nd = len(shape)
    return pl.BlockSpec(shape, lambda *_: (0,) * nd, pipeline_mode=pl.Buffered(1))


def _even_in_kernel(x_ref, nw_ref, w_ref, cos_ref, sin_ref,
                    q_ref, k_ref, v_ref, bg_ref, u_ref):
    xn = _rms(x_ref[...], nw_ref[...]).astype(BF16)
    cos = cos_ref[...]
    sin = sin_ref[...]
    lane = lax.broadcasted_iota(jnp.int32, cos.shape, 1)
    first_half = (lane % HEAD_DIM) < (HEAD_DIM // 2)

    def proj(c):
        return jnp.dot(xn, w_ref[:, c * A_QK:(c + 1) * A_QK],
                       preferred_element_type=F32)

    def rope(z, out_ref):
        for s in range(A_QK // LANES):
            zs = z[:, s * LANES:(s + 1) * LANES]
            swapped = jnp.where(first_half,
                                pltpu.roll(zs, LANES - HEAD_DIM // 2, 1),
                                pltpu.roll(zs, HEAD_DIM // 2, 1))
            out_ref[:, s * LANES:(s + 1) * LANES] = zs * cos + swapped * sin

    rope(proj(0), q_ref)
    rope(proj(1), k_ref)
    v_ref[...] = proj(2)
    bg_ref[...] = proj(3)
    u_ref[...] = proj(4) * proj(5)


def _even_in(x, nw, w, cos, sin, tm):
    n = x.shape[0]
    row = lambda i: (i, 0)
    out = jax.ShapeDtypeStruct((n, A_QK), F32)
    return pl.pallas_call(
        _even_in_kernel,
        grid=(n // tm,),
        in_specs=[pl.BlockSpec((tm, D_MODEL), row),
                  _const_spec((1, D_MODEL)),
                  _const_spec((D_MODEL, EVEN_IN)),
                  pl.BlockSpec((tm, LANES), row),
                  pl.BlockSpec((tm, LANES), row)],
        out_specs=[pl.BlockSpec((tm, A_QK), row)] * 5,
        out_shape=[out] * 5,
        compiler_params=_cparams(("parallel",)),
        name="even_in",
    )(x, nw, w, cos, sin)


def _flash_kernel(lam_ref, q_ref, k_ref, v_ref, sw_ref, o_ref,
                  qs_ref, m_ref, l_ref, acc_ref, *, tq, lam_scale):
    i = pl.program_id(2)
    j = pl.program_id(3)

    @pl.when(j == 0)
    def _init():
        q = q_ref[...] * ATTN_SCALE
        lane = lax.broadcasted_iota(jnp.int32, q.shape, 1)
        qs_ref[0:tq, :] = jnp.where(lane < HEAD_DIM, q, 0.0).astype(BF16)
        qs_ref[tq:2 * tq, :] = jnp.where(lane >= HEAD_DIM, q, 0.0).astype(BF16)
        m_ref[...] = jnp.full(m_ref.shape, -jnp.inf, F32)
        l_ref[...] = jnp.zeros(l_ref.shape, F32)
        acc_ref[...] = jnp.zeros(acc_ref.shape, F32)

    def step(masked):
        k = k_ref[...].astype(BF16)
        v = v_ref[...].astype(BF16)
        s = lax.dot_general(qs_ref[...], k, (((1,), (1,)), ((), ())),
                            preferred_element_type=F32)
        if masked:
            row = lax.broadcasted_iota(jnp.int32, s.shape, 0) % tq
            col = lax.broadcasted_iota(jnp.int32, s.shape, 1)
            s = jnp.where(col <= row, s, -jnp.inf)
        m_prev = m_ref[...]
        m_new = jnp.maximum(m_prev, jnp.max(s, axis=-1, keepdims=True))
        alpha = jnp.exp(m_prev - m_new)
        p = jnp.exp(s - pltpu.repeat(m_new, s.shape[1] // LANES, axis=1))
        l_ref[...] = alpha * l_ref[...] + jnp.sum(p, axis=-1, keepdims=True)
        acc_ref[...] = alpha * acc_ref[...] + jnp.dot(
            p.astype(BF16), v, preferred_element_type=F32)
        m_ref[...] = m_new

    @pl.when(j < i)
    def _off_diag():
        step(False)

    @pl.when(j == i)
    def _diag():
        step(True)
        a = acc_ref[...] / l_ref[...]
        o = a[0:tq, :] - lam_ref[0] * a[tq:2 * tq, :]
        o_ref[...] = _rms(o, sw_ref[...]) * lam_scale


def _flash(lam, q, k, v, sw, nb, t, lam_scale, tq):
    nq = t // tq
    kern = functools.partial(_flash_kernel, tq=tq, lam_scale=lam_scale)
    qmap = lambda b, h, i, j: (b * nq + i, h)
    kmap = lambda b, h, i, j: (b * nq + jnp.minimum(j, i), h)
    return pl.pallas_call(
        kern,
        grid=(nb, A_HEADS, nq, nq),
        in_specs=[pl.BlockSpec(memory_space=pltpu.SMEM),
                  pl.BlockSpec((tq, LANES), qmap),
                  pl.BlockSpec((tq, LANES), kmap),
                  pl.BlockSpec((tq, LANES), kmap),
                  pl.BlockSpec((1, LANES), lambda b, h, i, j: (0, 0))],
        out_specs=pl.BlockSpec((tq, LANES), qmap),
        out_shape=jax.ShapeDtypeStruct((nb * t, A_V), F32),
        scratch_shapes=[pltpu.VMEM((2 * tq, LANES), BF16),
                        pltpu.VMEM((2 * tq, LANES), F32),
                        pltpu.VMEM((2 * tq, LANES), F32),
                        pltpu.VMEM((2 * tq, LANES), F32)],
        compiler_params=_cparams(("parallel", "parallel", "parallel", "arbitrary")),
        name="flash_diff_attn",
    )(lam, q, k, v, sw)


DEC_ROWS = 16


def _decode_attn_kernel(pt_ref, lam_ref, q_ref, kc_ref, vc_ref, sw_ref, *refs,
                        pages, lam_scale):
    del pt_ref
    kp = refs[0:pages]
    vp = refs[pages:2 * pages]
    o_ref = refs[2 * pages]
    qbd_ref, m_ref, l_ref, acc_ref = refs[2 * pages + 1:]
    j = pl.program_id(1)

    @pl.when(j == 0)
    def _init():
        q = jnp.broadcast_to(q_ref[0] * ATTN_SCALE, (DEC_ROWS, A_QK))
        row = lax.broadcasted_iota(jnp.int32, q.shape, 0)
        blk = lax.broadcasted_iota(jnp.int32, q.shape, 1) // HEAD_DIM
        qbd_ref[...] = jnp.where(row == blk, q, 0.0)
        m_ref[...] = jnp.full(m_ref.shape, -jnp.inf, F32)
        l_ref[...] = jnp.zeros(l_ref.shape, F32)
        acc_ref[...] = jnp.zeros(acc_ref.shape, F32)

    qbd = qbd_ref[...].astype(BF16)
    s = jnp.concatenate(
        [lax.dot_general(qbd, kp[p][...].astype(BF16), (((1,), (1,)), ((), ())),
                         preferred_element_type=F32) for p in range(pages)],
        axis=1)
    m_prev = m_ref[...]
    m_new = jnp.maximum(m_prev, jnp.max(s, axis=-1, keepdims=True))
    alpha = jnp.exp(m_prev - m_new)
    p_all = jnp.exp(s - pltpu.repeat(m_new, pages, axis=1))
    l_ref[...] = alpha * l_ref[...] + jnp.sum(p_all, axis=-1, keepdims=True)
    pv = jnp.zeros(acc_ref.shape, F32)
    for p in range(pages):
        pv = pv + jnp.dot(p_all[:, p * PAGE_SIZE:(p + 1) * PAGE_SIZE].astype(BF16),
                          vp[p][...].astype(BF16), preferred_element_type=F32)
    acc_ref[...] = pltpu.repeat(alpha, A_V // LANES, axis=1) * acc_ref[...] + pv
    m_ref[...] = m_new

    @pl.when(j == pl.num_programs(1) - 1)
    def _finish():
        s_cur = jnp.sum(qbd_ref[...] * kc_ref[0], axis=-1, keepdims=True)
        m_old = m_ref[...]
        m_fin = jnp.maximum(m_old, s_cur)
        a_fin = jnp.exp(m_old - m_fin)
        p_cur = jnp.exp(s_cur - m_fin)
        l_fin = a_fin * l_ref[...] + p_cur
        acc = (pltpu.repeat(a_fin, A_V // LANES, axis=1) * acc_ref[...]
               + pltpu.repeat(p_cur, A_V // LANES, axis=1) * vc_ref[0])
        a = acc / pltpu.repeat(l_fin, A_V // LANES, axis=1)
        lam = lam_ref[0]
        sw = sw_ref[...]
        for h in range(A_HEADS):
            cols = slice(h * A_VDIM, (h + 1) * A_VDIM)
            o = a[2 * h:2 * h + 1, cols] - lam * a[2 * h + 1:2 * h + 2, cols]
            o_ref[0, :, cols] = _rms(o, sw) * lam_scale


def _decode_attn(page_table, lam, qd, kd, vd, sw, cache_k, cache_v, e, lam_scale,
                 pages):
    nb, n_pages = page_table.shape
    kern = functools.partial(_decode_attn_kernel, pages=pages, lam_scale=lam_scale)
    tok = lambda b, j, pt: (b, 0, 0)

    def page_spec(p):
        return pl.BlockSpec((None, None, PAGE_SIZE, A_QK),
                            lambda b, j, pt: (e, pt[b, j * pages + p], 0, 0))

    grid_spec = pltpu.PrefetchScalarGridSpec(
        num_scalar_prefetch=1,
        grid=(nb, n_pages // pages),
        in_specs=([pl.BlockSpec(memory_space=pltpu.SMEM),
                   pl.BlockSpec((1, 1, A_QK), tok),
                   pl.BlockSpec((1, 1, A_QK), tok),
                   pl.BlockSpec((1, 1, A_V), tok),
                   pl.BlockSpec((1, LANES), lambda b, j, pt: (0, 0))]
                  + [page_spec(p) for p in range(pages)] * 2),
        out_specs=pl.BlockSpec((1, 1, A_V), tok),
        scratch_shapes=[pltpu.VMEM((DEC_ROWS, A_QK), F32),
                        pltpu.VMEM((DEC_ROWS, LANES), F32),
                        pltpu.VMEM((DEC_ROWS, LANES), F32),
                        pltpu.VMEM((DEC_ROWS, A_V), F32)])
    return pl.pallas_call(
        kern,
        grid_spec=grid_spec,
        out_shape=jax.ShapeDtypeStruct((nb, 1, A_V), F32),
        compiler_params=_cparams(("parallel", "arbitrary")),
        name="decode_diff_attn",
    )(page_table, lam, qd, kd, vd, sw, *([cache_k] * pages), *([cache_v] * pages))


def _even_out_kernel(*refs, tm, seq_tiles, explicit_prev):
    if explicit_prev:
        (o_ref, bg_ref, u_ref, um1_ref, um2_ref, cw_ref, w_ref, x_ref, nw_ref,
         out_ref) = refs
        u = u_ref[...]
        um1 = um1_ref[...]
        um2 = um2_ref[...]
    else:
        o_ref, bg_ref, u_ref, halo_ref, cw_ref, w_ref, x_ref, nw_ref, out_ref = refs
        u = u_ref[...]
        halo = halo_ref[...]
        halo = jnp.where(pl.program_id(0) % seq_tiles == 0, 0.0, halo)
        row = lax.broadcasted_iota(jnp.int32, u.shape, 0)
        h1 = halo[SUBLANES - 1:SUBLANES, :]
        h2 = halo[SUBLANES - 2:SUBLANES - 1, :]
        um1 = jnp.where(row == 0, h1, pltpu.roll(u, 1, 0))
        um2 = jnp.where(row == 0, h2, jnp.where(row == 1, h1, pltpu.roll(u, 2, 0)))
    cw = cw_ref[...]
    conv = um2 * cw[0:1, :] + um1 * cw[1:2, :] + u * cw[2:3, :]
    yb = bg_ref[...] * conv
    y = (jnp.dot(o_ref[...].astype(BF16), w_ref[0:A_V, :], preferred_element_type=F32)
         + jnp.dot(yb.astype(BF16), w_ref[A_V:A_V + B_WIDTH, :],
                   preferred_element_type=F32))
    out_ref[...] = x_ref[...] + _rms(y, nw_ref[...])


def _even_out(o, bg, u, prev, cw, w, x, nw, tm, seq_len):
    n = x.shape[0]
    row = lambda i: (i, 0)
    half = pl.BlockSpec((tm, B_WIDTH), row)
    explicit_prev = isinstance(prev, tuple)
    if explicit_prev:
        prev_args = list(prev)
        prev_specs = [half, half]
    else:
        prev_args = [u]
        prev_specs = [pl.BlockSpec(
            (SUBLANES, B_WIDTH),
            lambda i: (jnp.maximum(i * (tm // SUBLANES) - 1, 0), 0))]
    kern = functools.partial(_even_out_kernel, tm=tm, seq_tiles=seq_len // tm,
                             explicit_prev=explicit_prev)
    return pl.pallas_call(
        kern,
        grid=(n // tm,),
        in_specs=([half, half, half] + prev_specs
                  + [_const_spec((CONV_W, B_WIDTH)),
                     _const_spec((D_MODEL, D_MODEL)),
                     pl.BlockSpec((tm, D_MODEL), row),
                     _const_spec((1, D_MODEL))]),
        out_specs=pl.BlockSpec((tm, D_MODEL), row),
        out_shape=jax.ShapeDtypeStruct((n, D_MODEL), F32),
        compiler_params=_cparams(("parallel",)),
        name="even_out",
    )(o, bg, u, *prev_args, cw, w, x, nw)


def _ffn_kernel(x_ref, nw_in_ref, wi_ref, wo_ref, nw_out_ref, out_ref):
    x = x_ref[...]
    xn = _rms(x, nw_in_ref[...]).astype(BF16)
    y = jnp.zeros(x.shape, F32)
    for c in range(D_FF // FFN_CHUNK):
        lo = c * FFN_CHUNK
        gate = jnp.dot(xn, wi_ref[:, lo:lo + FFN_CHUNK], preferred_element_type=F32)
        up = jnp.dot(xn, wi_ref[:, D_FF + lo:D_FF + lo + FFN_CHUNK],
                     preferred_element_type=F32)
        act = (gate * _sigmoid(gate) * up).astype(BF16)
        y = y + jnp.dot(act, wo_ref[lo:lo + FFN_CHUNK, :], preferred_element_type=F32)
    out_ref[...] = x + _rms(y, nw_out_ref[...])


def _ffn(x, nw_in, wi, wo, nw_out, tm):
    n = x.shape[0]
    row = lambda i: (i, 0)
    return pl.pallas_call(
        _ffn_kernel,
        grid=(n // tm,),
        in_specs=[pl.BlockSpec((tm, D_MODEL), row),
                  _const_spec((1, D_MODEL)),
                  _const_spec((D_MODEL, 2 * D_FF)),
                  _const_spec((D_FF, D_MODEL)),
                  _const_spec((1, D_MODEL))],
        out_specs=pl.BlockSpec((tm, D_MODEL), row),
        out_shape=jax.ShapeDtypeStruct((n, D_MODEL), F32),
        compiler_params=_cparams(("parallel",)),
        name="ffn",
    )(x, nw_in, wi, wo, nw_out)


def _odd_in_kernel(x_ref, nw_ref, w_ref, z_ref):
    xn = _rms(x_ref[...], nw_ref[...]).astype(BF16)
    for c in range(ODD_IN // D_MODEL):
        cols = slice(c * D_MODEL, (c + 1) * D_MODEL)
        z_ref[:, cols] = jnp.dot(xn, w_ref[:, cols], preferred_element_type=F32)


def _odd_in(x, nw, w, tm):
    n = x.shape[0]
    row = lambda i: (i, 0)
    return pl.pallas_call(
        _odd_in_kernel,
        grid=(n // tm,),
        in_specs=[pl.BlockSpec((tm, D_MODEL), row),
                  _const_spec((1, D_MODEL)),
                  _const_spec((D_MODEL, ODD_IN))],
        out_specs=pl.BlockSpec((tm, ODD_IN), row),
        out_shape=jax.ShapeDtypeStruct((n, ODD_IN), F32),
        compiler_params=_cparams(("parallel",)),
        name="odd_in",
    )(x, nw, w)


def _gla_kernel(zq_ref, zf_ref, zi_ref, zg_ref, lb_ref, gw_ref, s0_ref,
                tril_ref, ones_ref, og_ref, sout_ref,
                st_ref, q_s, k_s, v_s, g_s, qh_s, kh_s, dec_s, o_s,
                *, rows, t_valid, unroll):
    tb = pl.program_id(2)

    @pl.when(tb == 0)
    def _load_state():
        st_ref[...] = s0_ref[0, 0].T

    zq = zq_ref[...]
    q = zq * _sigmoid(zq)
    lb = lb_ref[...]
    fg = lb + (1.0 - lb) * _sigmoid(zf_ref[...])
    logf = jnp.log(fg)
    kk = 1.0 - fg
    if t_valid is not None:
        valid = lax.broadcasted_iota(jnp.int32, logf.shape, 0) < t_valid
        logf = jnp.where(valid, logf, 0.0)
        kk = jnp.where(valid, kk, 0.0)
    v = zi_ref[...]
    hi = lax.Precision.HIGHEST
    g = jnp.dot(tril_ref[...], logf, precision=hi, preferred_element_type=F32)
    gtot = jnp.dot(ones_ref[...], logf, precision=hi, preferred_element_type=F32)
    q_s[...] = q
    k_s[...] = kk
    v_s[...] = v
    g_s[...] = g
    qh_s[...] = q * jnp.exp(g)
    kh_s[...] = kk * jnp.exp(gtot - g)
    dec_s[...] = jnp.exp(gtot)

    s_idx = lax.broadcasted_iota(jnp.int32, (GLA_SUB, HG_DK), 0)

    def block(i, st):
        r0 = pl.multiple_of(i * GLA_SUB, GLA_SUB)
        sl = pl.ds(r0, GLA_SUB)
        qi = q_s[sl, :]
        ki = k_s[sl, :]
        vi = v_s[sl, :]
        gi = g_s[sl, :]
        o_inter = lax.dot_general(qh_s[sl, :].astype(BF16), st.astype(BF16),
                                  (((1,), (1,)), ((), ())),
                                  preferred_element_type=F32)
        upd = lax.dot_general(vi.astype(BF16), kh_s[sl, :].astype(BF16),
                              (((0,), (0,)), ((), ())),
                              preferred_element_type=F32)
        st_new = st * dec_s[pl.ds(r0, 1), :] + upd
        out_rows = []
        for t in range(GLA_SUB):
            d = gi[t:t + 1, :] - gi
            e = jnp.exp(jnp.where(s_idx <= t, d, -jnp.inf))
            a = jnp.sum(qi[t:t + 1, :] * e * ki, axis=-1, keepdims=True)
            out_rows.append(jnp.sum(a * vi, axis=0, keepdims=True))
        o_s[sl, :] = o_inter + jnp.concatenate(out_rows, axis=0)
        return st_new

    st = lax.fori_loop(0, rows // GLA_SUB, block, st_ref[...], unroll=unroll)
    st_ref[...] = st
    zg = zg_ref[...]
    og_ref[...] = _rms(o_s[...], gw_ref[...]) * (zg * _sigmoid(zg))

    @pl.when(tb == pl.num_programs(2) - 1)
    def _store_state():
        sout_ref[0, 0] = st.T


def _gla(z, lb, gw, s0, nseq, t_pad, t_valid, rows, unroll):
    nt = t_pad // rows
    blk_id = jnp.arange(rows) // GLA_SUB
    same_blk = blk_id[:, None] == blk_id[None, :]
    tril = (same_blk & (jnp.arange(rows)[None, :] <= jnp.arange(rows)[:, None])).astype(F32)
    ones = same_blk.astype(F32)

    def col_spec(base):
        return pl.BlockSpec((rows, HG_DK),
                            lambda s, h, t: (s * nt + t, base + h))

    head_vec = pl.BlockSpec((1, HG_DK), lambda s, h, t: (0, h))
    state = pl.BlockSpec((1, 1, HG_DK, HG_DV), lambda s, h, t: (s, h, 0, 0))
    sq = pl.BlockSpec((rows, rows), lambda s, h, t: (0, 0))
    kern = functools.partial(_gla_kernel, rows=rows, t_valid=t_valid, unroll=unroll)
    tile = pltpu.VMEM((rows, HG_DK), F32)
    return pl.pallas_call(
        kern,
        grid=(nseq, HG_HEADS, nt),
        in_specs=[col_spec(0), col_spec(HG_HEADS), col_spec(2 * HG_HEADS),
                  col_spec(3 * HG_HEADS), head_vec,
                  pl.BlockSpec((1, HG_DV), lambda s, h, t: (0, 0)),
                  state, sq, sq],
        out_specs=[pl.BlockSpec((rows, HG_DV), lambda s, h, t: (s * nt + t, h)),
                   state],
        out_shape=[jax.ShapeDtypeStruct((nseq * t_pad, D_MODEL), F32),
                   jax.ShapeDtypeStruct((nseq, HG_HEADS, HG_DK, HG_DV), F32)],
        scratch_shapes=[pltpu.VMEM((HG_DV, HG_DK), F32)] + [tile] * 8,
        compiler_params=_cparams(("parallel", "parallel", "arbitrary")),
        name="hgrn2_gla",
    )(z, z, z, z, lb, gw, s0, tril, ones)


def _odd_out_kernel(a_ref, w_ref, x_ref, nw_ref, out_ref):
    y = jnp.dot(a_ref[...].astype(BF16), w_ref[...], preferred_element_type=F32)
    out_ref[...] = x_ref[...] + _rms(y, nw_ref[...])


def _odd_out(a, w, x, nw, tm):
    n = x.shape[0]
    row = lambda i: (i, 0)
    return pl.pallas_call(
        _odd_out_kernel,
        grid=(n // tm,),
        in_specs=[pl.BlockSpec((tm, D_MODEL), row),
                  _const_spec((D_MODEL, D_MODEL)),
                  pl.BlockSpec((tm, D_MODEL), row),
                  _const_spec((1, D_MODEL))],
        out_specs=pl.BlockSpec((tm, D_MODEL), row),
        out_shape=jax.ShapeDtypeStruct((n, D_MODEL), F32),
        compiler_params=_cparams(("parallel",)),
        name="odd_out",
    )(a, w, x, nw)


def _rope_tables(pos):
    half = HEAD_DIM // 2
    inv = ROPE_THETA ** (-jnp.arange(half, dtype=F32) / half)
    ang = pos.astype(F32)[:, None] * inv[None, :]
    cos = jnp.tile(jnp.cos(ang), (1, LANES // half))
    sin = jnp.tile(jnp.concatenate([-jnp.sin(ang), jnp.sin(ang)], axis=1),
                   (1, LANES // HEAD_DIM))
    return cos, sin


def kernel(x_prompt, x_sample, cache_k, cache_v, state_conv, state_hgrn, page_table,
           norm_w, w_in_even, w_out_even, conv_w, lambda_q1, lambda_k1, lambda_q2,
           lambda_k2, subln_w, w_in_odd, w_out_odd, hgrn_lb, gnorm_w, w_ffn_in,
           w_ffn_out):
    nb, t, _ = x_prompt.shape
    nd, td, _ = x_sample.shape
    assert td == 1
    depth = norm_w.shape[0]
    n_pages = page_table.shape[1]
    past_len = n_pages * PAGE_SIZE
    tm_p = 512
    tm_d = nd
    dec_pad = GLA_SUB
    dec_pages = 8

    cos_p, sin_p = _rope_tables(jnp.tile(jnp.arange(t), nb))
    cos_d, sin_d = _rope_tables(jnp.full((nd,), past_len))
    p_lb = jax.nn.softmax(hgrn_lb.astype(F32), axis=0)
    lower_bounds = jnp.cumsum(p_lb, axis=0) - p_lb[0]

    ck = cache_k.reshape(cache_k.shape[0], cache_k.shape[1], PAGE_SIZE, A_QK)
    cv = cache_v.reshape(cache_v.shape[0], cache_v.shape[1], PAGE_SIZE, A_V)
    nw = norm_w.reshape(depth, 4, 1, D_MODEL)

    xp = x_prompt.reshape(nb * t, D_MODEL)
    xs = x_sample.reshape(nd, D_MODEL)
    kp_l, vp_l, cp_l, hp_l = [], [], [], []
    ks_l, vs_l, cs_l, hs_l = [], [], [], []
    for layer in range(depth):
        if layer % 2 == 0:
            e = layer // 2
            lam_init = 0.8 - 0.6 * math.exp(-0.3 * layer)
            lam = (jnp.exp(jnp.sum(lambda_q1[e].astype(F32) * lambda_k1[e].astype(F32)))
                   - jnp.exp(jnp.sum(lambda_q2[e].astype(F32) * lambda_k2[e].astype(F32)))
                   + lam_init).reshape(1)
            w_in = w_in_even[e].astype(BF16)
            w_out = w_out_even[e].astype(BF16)
            sw = subln_w[e].reshape(1, A_VDIM)
            cw = conv_w[e]

            q, k, v, bg, u = _even_in(xp, nw[layer, 0], w_in, cos_p, sin_p, tm_p)
            o = _flash(lam, q, k, v, sw, nb, t, 1.0 - lam_init, tm_p)
            yp_in = (o, bg, u)
            xp_mid = _even_out(*yp_in, None, cw, w_out, xp, nw[layer, 1], tm_p, t)

            qd, kd, vd, bgd, ud = _even_in(xs, nw[layer, 0], w_in, cos_d, sin_d, tm_d)
            od = _decode_attn(page_table, lam, qd.reshape(nd, 1, A_QK),
                              kd.reshape(nd, 1, A_QK), vd.reshape(nd, 1, A_V), sw,
                              ck, cv, e, 1.0 - lam_init, dec_pages)
            sc = state_conv[e]
            xs_mid = _even_out(od.reshape(nd, A_V), bgd, ud, (sc[:, 1], sc[:, 0]), cw,
                               w_out, xs, nw[layer, 1], tm_d, td)

            kp_l.append(k.reshape(nb, t, A_HEADS, 2, HEAD_DIM))
            vp_l.append(v.reshape(nb, t, A_HEADS, A_VDIM))
            cp_l.append(u.reshape(nb, t, B_WIDTH)[:, t - (CONV_W - 1):])
            ks_l.append(kd.reshape(nd, td, A_HEADS, 2, HEAD_DIM))
            vs_l.append(vd.reshape(nd, td, A_HEADS, A_VDIM))
            cs_l.append(jnp.concatenate([sc[:, 1:], ud[:, None, :]], axis=1))
        else:
            o_idx = layer // 2
            w_in = w_in_odd[o_idx].astype(BF16)
            w_out = w_out_odd[o_idx].astype(BF16)
            lb = lower_bounds[layer].reshape(1, HG_FDIM)
            gw = gnorm_w[o_idx].reshape(1, HG_DV)

            z = _odd_in(xp, nw[layer, 0], w_in, tm_p)
            s0 = jnp.zeros((nb, HG_HEADS, HG_DK, HG_DV), F32)
            og, sp = _gla(z, lb, gw, s0, nb, t, None, 256, 2)
            xp_mid = _odd_out(og, w_out, xp, nw[layer, 1], tm_p)

            zd = _odd_in(xs, nw[layer, 0], w_in, tm_d)
            zd = jnp.pad(zd[:, None, :], ((0, 0), (0, dec_pad - td), (0, 0)))
            ogd, ss = _gla(zd.reshape(nd * dec_pad, ODD_IN), lb, gw, state_hgrn[o_idx],
                           nd, dec_pad, td, dec_pad, 1)
            ogd = ogd.reshape(nd, dec_pad, D_MODEL)[:, 0]
            xs_mid = _odd_out(ogd, w_out, xs, nw[layer, 1], tm_d)
            hp_l.append(sp)
            hs_l.append(ss)

        wi = w_ffn_in[layer].astype(BF16)
        wo = w_ffn_out[layer].astype(BF16)
        xp = _ffn(xp_mid, nw[layer, 2], wi, wo, nw[layer, 3], tm_p)
        xs = _ffn(xs_mid, nw[layer, 2], wi, wo, nw[layer, 3], tm_d)

    return (xp.reshape(nb, t, D_MODEL), xs.reshape(nd, td, D_MODEL),
            jnp.stack(kp_l), jnp.stack(vp_l), jnp.stack(cp_l), jnp.stack(hp_l),
            jnp.stack(ks_l), jnp.stack(vs_l), jnp.stack(cs_l), jnp.stack(hs_l))
```

```python
import functools
import math

import numpy as np
import jax
import jax.numpy as jnp
from jax import lax
from jax.experimental import pallas as pl
from jax.experimental.pallas import tpu as pltpu

D_MODEL = 1024
PAGE_SIZE = 128
HEAD_DIM = 64
A_HEADS = 4
A_QK = A_HEADS * 2 * HEAD_DIM
A_VDIM = 2 * HEAD_DIM
A_V = A_HEADS * A_VDIM
ATTN_SCALE = HEAD_DIM ** -0.5
ROPE_THETA = 10000.0
B_WIDTH = D_MODEL // 2
CONV_W = 3
EVEN_IN = 2 * A_QK + A_V + 3 * B_WIDTH
HG_HEADS = 8
HG_DK = 128
HG_FDIM = HG_HEADS * HG_DK
HG_DV = D_MODEL // HG_HEADS
ODD_IN = 2 * HG_FDIM + 2 * D_MODEL
D_FF = -(-8 * D_MODEL // (3 * 256)) * 256
EPS = 1e-6
LOG2E = math.log2(math.e)

LANES = 128
SUBLANES = 8
VMEM_LIMIT_BYTES = 56 * 1024 * 1024
GLA_CHUNK = 128
GLA_LEVELS = 7
FFN_CHUNK = 256
FLASH_ROWS = 128

F32 = jnp.float32
BF16 = jnp.bfloat16


def _cparams(semantics):
    return pltpu.CompilerParams(dimension_semantics=semantics,
                                vmem_limit_bytes=VMEM_LIMIT_BYTES)


def _rms(x, w):
    return x * lax.rsqrt(jnp.mean(x * x, axis=-1, keepdims=True) + EPS) * w


def _sigmoid(x):
    return 1.0 / (1.0 + jnp.exp(-x))


def _const_spec(shape):
    nd = len(shape)
    return pl.BlockSpec(shape, lambda *_: (0,) * nd, pipeline_mode=pl.Buffered(1))


def _even_in_kernel(x_ref, nw_ref, w_ref, cos_ref, sin_ref,
                    q_ref, k_ref, v_ref, bg_ref, u_ref):
    xn = _rms(x_ref[...], nw_ref[...]).astype(BF16)
    cos = cos_ref[...]
    sin = sin_ref[...]
    lane = lax.broadcasted_iota(jnp.int32, cos.shape, 1)
    first_half = (lane % HEAD_DIM) < (HEAD_DIM // 2)

    def proj(c):
        return jnp.dot(xn, w_ref[:, c * A_QK:(c + 1) * A_QK],
                       preferred_element_type=F32)

    def rope(z, out_ref):
        for s in range(A_QK // LANES):
            zs = z[:, s * LANES:(s + 1) * LANES]
            swapped = jnp.where(first_half,
                                pltpu.roll(zs, LANES - HEAD_DIM // 2, 1),
                                pltpu.roll(zs, HEAD_DIM // 2, 1))
            out_ref[:, s * LANES:(s + 1) * LANES] = zs * cos + swapped * sin

    rope(proj(0), q_ref)
    rope(proj(1), k_ref)
    v_ref[...] = proj(2)
    bg_ref[...] = proj(3)
    u_ref[...] = proj(4) * proj(5)


def _even_in(x, nw, w, cos, sin, tm):
    n = x.shape[0]
    row = lambda i: (i, 0)
    out = jax.ShapeDtypeStruct((n, A_QK), F32)
    return pl.pallas_call(
        _even_in_kernel,
        grid=(n // tm,),
        in_specs=[pl.BlockSpec((tm, D_MODEL), row),
                  _const_spec((1, D_MODEL)),
                  _const_spec((D_MODEL, EVEN_IN)),
                  pl.BlockSpec((tm, LANES), row),
                  pl.BlockSpec((tm, LANES), row)],
        out_specs=[pl.BlockSpec((tm, A_QK), row)] * 5,
        out_shape=[out] * 5,
        compiler_params=_cparams(("parallel",)),
        name="even_in",
    )(x, nw, w, cos, sin)


def _flash_kernel(ii_ref, jj_ref, lam_ref, q_ref, k_ref, v_ref, sw_ref, o_ref,
                  qs_ref, m_ref, acc_ref, *, tq, lam_scale):
    pair = pl.program_id(2)
    i = ii_ref[pair]
    j = jj_ref[pair]

    @pl.when(j == 0)
    def _init():
        q = q_ref[...] * (ATTN_SCALE * LOG2E)
        lane = lax.broadcasted_iota(jnp.int32, q.shape, 1)
        qs_ref[...] = q.astype(BF16)
        m_ref[...] = jnp.full(m_ref.shape, -jnp.inf, F32)
        acc_ref[...] = jnp.zeros(acc_ref.shape, F32)

    def step(masked):
        kf = k_ref[...]
        lane = lax.broadcasted_iota(jnp.int32, kf.shape, 1)
        v = v_ref[...].astype(BF16)
        v_ext = jnp.concatenate([v, jnp.ones_like(v)], axis=1)
        for mp in range(2):
            keep = (lane < HEAD_DIM) if mp == 0 else (lane >= HEAD_DIM)
            k = jnp.where(keep, kf, 0.0).astype(BF16)
            for rc in range(tq // FLASH_ROWS):
                rows = slice(rc * FLASH_ROWS, (rc + 1) * FLASH_ROWS)
                s = lax.dot_general(qs_ref[rows, :], k, (((1,), (1,)), ((), ())),
                                    preferred_element_type=F32)
                if masked:
                    row = lax.broadcasted_iota(jnp.int32, s.shape, 0) + rc * FLASH_ROWS
                    col = lax.broadcasted_iota(jnp.int32, s.shape, 1)
                    s = jnp.where(col <= row, s, -jnp.inf)
                m_prev = m_ref[mp, rows, :]
                m_new = jnp.maximum(m_prev, jnp.max(s, axis=-1, keepdims=True))
                alpha = jnp.exp2(m_prev - m_new)
                p = jnp.exp2(s - pltpu.repeat(m_new, s.shape[1] // LANES, axis=1))
                pv = jnp.dot(p.astype(BF16), v_ext, preferred_element_type=F32)
                acc_ref[mp, rows, :] = (pltpu.repeat(alpha, 2, axis=1)
                                        * acc_ref[mp, rows, :] + pv)
                m_ref[mp, rows, :] = m_new

    @pl.when(j < i)
    def _off_diag():
        step(False)

    @pl.when(j == i)
    def _diag():
        step(True)
        a0 = acc_ref[0]
        a1 = acc_ref[1]
        o = (a0[:, 0:LANES] / a0[:, LANES:2 * LANES]
             - lam_ref[0] * (a1[:, 0:LANES] / a1[:, LANES:2 * LANES]))
        o_ref[...] = _rms(o, sw_ref[...]) * lam_scale


def _flash(lam, q, k, v, sw, nb, t, lam_scale, tq):
    nq = t // tq
    pairs = [(i, j) for i in range(nq) for j in range(i + 1)]
    ii = jnp.asarray(np.array([p[0] for p in pairs], np.int32))
    jj = jnp.asarray(np.array([p[1] for p in pairs], np.int32))
    kern = functools.partial(_flash_kernel, tq=tq, lam_scale=lam_scale)
    qmap = lambda b, h, p, ii, jj: (b * nq + ii[p], h)
    kmap = lambda b, h, p, ii, jj: (b * nq + jj[p], h)
    grid_spec = pltpu.PrefetchScalarGridSpec(
        num_scalar_prefetch=2,
        grid=(nb, A_HEADS, len(pairs)),
        in_specs=[pl.BlockSpec(memory_space=pltpu.SMEM),
                  pl.BlockSpec((tq, LANES), qmap),
                  pl.BlockSpec((tq, LANES), kmap),
                  pl.BlockSpec((tq, LANES), kmap),
                  pl.BlockSpec((1, LANES), lambda b, h, p, ii, jj: (0, 0))],
        out_specs=pl.BlockSpec((tq, LANES), qmap),
        scratch_shapes=[pltpu.VMEM((tq, LANES), BF16),
                        pltpu.VMEM((2, tq, LANES), F32),
                        pltpu.VMEM((2, tq, 2 * LANES), F32)])
    return pl.pallas_call(
        kern,
        grid_spec=grid_spec,
        out_shape=jax.ShapeDtypeStruct((nb * t, A_V), F32),
        compiler_params=_cparams(("parallel", "parallel", "arbitrary")),
        name="flash_diff_attn",
    )(ii, jj, lam, q, k, v, sw)


DEC_ROWS = 16


def _decode_attn_kernel(pt_ref, lam_ref, q_ref, kc_ref, vc_ref, sw_ref, *refs,
                        pages, lam_scale):
    del pt_ref
    kp = refs[0:pages]
    vp = refs[pages:2 * pages]
    o_ref = refs[2 * pages]
    qbd_ref, m_ref, l_ref, acc_ref = refs[2 * pages + 1:]
    j = pl.program_id(1)

    @pl.when(j == 0)
    def _init():
        q = jnp.broadcast_to(q_ref[0] * (ATTN_SCALE * LOG2E), (DEC_ROWS, A_QK))
        row = lax.broadcasted_iota(jnp.int32, q.shape, 0)
        blk = lax.broadcasted_iota(jnp.int32, q.shape, 1) // HEAD_DIM
        qbd_ref[...] = jnp.where(row == blk, q, 0.0)
        m_ref[...] = jnp.full(m_ref.shape, -jnp.inf, F32)
        l_ref[...] = jnp.zeros(l_ref.shape, F32)
        acc_ref[...] = jnp.zeros(acc_ref.shape, F32)

    qbd = qbd_ref[...].astype(BF16)
    s = jnp.concatenate(
        [jnp.dot(qbd, kp[p][...].astype(BF16), preferred_element_type=F32)
         for p in range(pages)], axis=1)
    m_prev = m_ref[...]
    m_new = jnp.maximum(m_prev, jnp.max(s, axis=-1, keepdims=True))
    alpha = jnp.exp2(m_prev - m_new)
    p_all = jnp.exp2(s - pltpu.repeat(m_new, pages, axis=1))
    l_ref[...] = alpha * l_ref[...] + jnp.sum(p_all, axis=-1, keepdims=True)
    p_bf = p_all.astype(BF16)
    for h in range(A_HEADS):
        pv = jnp.zeros((DEC_ROWS, A_VDIM), F32)
        for p in range(pages):
            v_h = vp[p][pl.ds(h, PAGE_SIZE, stride=A_HEADS), :]
            pv = pv + jnp.dot(p_bf[:, p * PAGE_SIZE:(p + 1) * PAGE_SIZE],
                              v_h.astype(BF16), preferred_element_type=F32)
        acc_ref[h] = alpha * acc_ref[h] + pv
    m_ref[...] = m_new

    @pl.when(j == pl.num_programs(1) - 1)
    def _finish():
        s_cur = jnp.sum(qbd_ref[...] * kc_ref[0], axis=-1, keepdims=True)
        m_old = m_ref[...]
        m_fin = jnp.maximum(m_old, s_cur)
        a_fin = jnp.exp2(m_old - m_fin)
        p_cur = jnp.exp2(s_cur - m_fin)
        l_fin = a_fin * l_ref[...] + p_cur
        lam = lam_ref[0]
        sw = sw_ref[...]
        vc = vc_ref[0]
        for h in range(A_HEADS):
            cols = slice(h * A_VDIM, (h + 1) * A_VDIM)
            a = (a_fin * acc_ref[h] + p_cur * vc[:, cols]) / l_fin
            o = a[2 * h:2 * h + 1, :] - lam * a[2 * h + 1:2 * h + 2, :]
            o_ref[0, :, cols] = _rms(o, sw) * lam_scale


def _decode_attn(page_table, lam, qd, kd, vd, sw, cache_kt, cache_vf, e, lam_scale,
                 pages):
    nb, n_pages = page_table.shape
    kern = functools.partial(_decode_attn_kernel, pages=pages, lam_scale=lam_scale)
    tok = lambda b, j, pt: (b, 0, 0)

    def page_spec(p):
        return pl.BlockSpec((None, None, A_QK, PAGE_SIZE),
                            lambda b, j, pt: (e, pt[b, j * pages + p], 0, 0))

    grid_spec = pltpu.PrefetchScalarGridSpec(
        num_scalar_prefetch=1,
        grid=(nb, n_pages // pages),
        in_specs=([pl.BlockSpec(memory_space=pltpu.SMEM),
                   pl.BlockSpec((1, 1, A_QK), tok),
                   pl.BlockSpec((1, 1, A_QK), tok),
                   pl.BlockSpec((1, 1, A_V), tok),
                   pl.BlockSpec((1, LANES), lambda b, j, pt: (0, 0))]
                  + [page_spec(p) for p in range(pages)] * 2),
        out_specs=pl.BlockSpec((1, 1, A_V), tok),
        scratch_shapes=[pltpu.VMEM((DEC_ROWS, A_QK), F32),
                        pltpu.VMEM((DEC_ROWS, LANES), F32),
                        pltpu.VMEM((DEC_ROWS, LANES), F32),
                        pltpu.VMEM((A_HEADS, DEC_ROWS, A_VDIM), F32)])
    return pl.pallas_call(
        kern,
        grid_spec=grid_spec,
        out_shape=jax.ShapeDtypeStruct((nb, 1, A_V), F32),
        compiler_params=_cparams(("parallel", "arbitrary")),
        name="decode_diff_attn",
    )(page_table, lam, qd, kd, vd, sw, *([cache_kt] * pages), *([cache_vf] * pages))


def _even_out_kernel(*refs, tm, seq_tiles, explicit_prev):
    if explicit_prev:
        (o_ref, bg_ref, u_ref, um1_ref, um2_ref, cw_ref, w_ref, x_ref, nw_ref,
         out_ref) = refs
        u = u_ref[...]
        um1 = um1_ref[...]
        um2 = um2_ref[...]
    else:
        o_ref, bg_ref, u_ref, halo_ref, cw_ref, w_ref, x_ref, nw_ref, out_ref = refs
        u = u_ref[...]
        halo = halo_ref[...]
        halo = jnp.where(pl.program_id(0) % seq_tiles == 0, 0.0, halo)
        row = lax.broadcasted_iota(jnp.int32, u.shape, 0)
        h1 = halo[SUBLANES - 1:SUBLANES, :]
        h2 = halo[SUBLANES - 2:SUBLANES - 1, :]
        um1 = jnp.where(row == 0, h1, pltpu.roll(u, 1, 0))
        um2 = jnp.where(row == 0, h2, jnp.where(row == 1, h1, pltpu.roll(u, 2, 0)))
    cw = cw_ref[...]
    conv = um2 * cw[0:1, :] + um1 * cw[1:2, :] + u * cw[2:3, :]
    yb = bg_ref[...] * conv
    y = (jnp.dot(o_ref[...].astype(BF16), w_ref[0:A_V, :], preferred_element_type=F32)
         + jnp.dot(yb.astype(BF16), w_ref[A_V:A_V + B_WIDTH, :],
                   preferred_element_type=F32))
    out_ref[...] = x_ref[...] + _rms(y, nw_ref[...])


def _even_out(o, bg, u, prev, cw, w, x, nw, tm, seq_len):
    n = x.shape[0]
    row = lambda i: (i, 0)
    half = pl.BlockSpec((tm, B_WIDTH), row)
    explicit_prev = isinstance(prev, tuple)
    if explicit_prev:
        prev_args = list(prev)
        prev_specs = [half, half]
    else:
        prev_args = [u]
        prev_specs = [pl.BlockSpec(
            (SUBLANES, B_WIDTH),
            lambda i: (jnp.maximum(i * (tm // SUBLANES) - 1, 0), 0))]
    kern = functools.partial(_even_out_kernel, tm=tm, seq_tiles=seq_len // tm,
                             explicit_prev=explicit_prev)
    return pl.pallas_call(
        kern,
        grid=(n // tm,),
        in_specs=([half, half, half] + prev_specs
                  + [_const_spec((CONV_W, B_WIDTH)),
                     _const_spec((D_MODEL, D_MODEL)),
                     pl.BlockSpec((tm, D_MODEL), row),
                     _const_spec((1, D_MODEL))]),
        out_specs=pl.BlockSpec((tm, D_MODEL), row),
        out_shape=jax.ShapeDtypeStruct((n, D_MODEL), F32),
        compiler_params=_cparams(("parallel",)),
        name="even_out",
    )(o, bg, u, *prev_args, cw, w, x, nw)


def _ffn_kernel(x_ref, nw_in_ref, wi_ref, wo_ref, nw_out_ref, out_ref):
    x = x_ref[...]
    xn = _rms(x, nw_in_ref[...]).astype(BF16)
    y = jnp.zeros(x.shape, F32)
    for c in range(D_FF // FFN_CHUNK):
        lo = c * FFN_CHUNK
        gate = jnp.dot(xn, wi_ref[:, lo:lo + FFN_CHUNK], preferred_element_type=F32)
        up = jnp.dot(xn, wi_ref[:, D_FF + lo:D_FF + lo + FFN_CHUNK],
                     preferred_element_type=F32)
        act = (gate * _sigmoid(gate) * up).astype(BF16)
        y = y + jnp.dot(act, wo_ref[lo:lo + FFN_CHUNK, :], preferred_element_type=F32)
    out_ref[...] = x + _rms(y, nw_out_ref[...])


def _ffn(x, nw_in, wi, wo, nw_out, tm):
    n = x.shape[0]
    row = lambda i: (i, 0)
    return pl.pallas_call(
        _ffn_kernel,
        grid=(n // tm,),
        in_specs=[pl.BlockSpec((tm, D_MODEL), row),
                  _const_spec((1, D_MODEL)),
                  _const_spec((D_MODEL, 2 * D_FF)),
                  _const_spec((D_FF, D_MODEL)),
                  _const_spec((1, D_MODEL))],
        out_specs=pl.BlockSpec((tm, D_MODEL), row),
        out_shape=jax.ShapeDtypeStruct((n, D_MODEL), F32),
        compiler_params=_cparams(("parallel",)),
        name="ffn",
    )(x, nw_in, wi, wo, nw_out)


def _odd_in_kernel(x_ref, nw_ref, w_ref, z_ref):
    xn = _rms(x_ref[...], nw_ref[...]).astype(BF16)
    for c in range(ODD_IN // D_MODEL):
        cols = slice(c * D_MODEL, (c + 1) * D_MODEL)
        z_ref[:, cols] = jnp.dot(xn, w_ref[:, cols], preferred_element_type=F32)


def _odd_in(x, nw, w, tm):
    n = x.shape[0]
    row = lambda i: (i, 0)
    return pl.pallas_call(
        _odd_in_kernel,
        grid=(n // tm,),
        in_specs=[pl.BlockSpec((tm, D_MODEL), row),
                  _const_spec((1, D_MODEL)),
                  _const_spec((D_MODEL, ODD_IN))],
        out_specs=pl.BlockSpec((tm, ODD_IN), row),
        out_shape=jax.ShapeDtypeStruct((n, ODD_IN), F32),
        compiler_params=_cparams(("parallel",)),
        name="odd_in",
    )(x, nw, w)


def _gla_tables():
    c = GLA_CHUNK
    t = np.arange(c)[:, None]
    r = np.arange(c)[None, :]
    mats = []
    for lev in range(GLA_LEVELS):
        half = 1 << lev
        mid = (t // (2 * half)) * (2 * half) + half
        mats.append(np.where(t >= mid, (r >= mid) & (r <= t), (r > t) & (r < mid)))
    mats.append(r <= t)
    mats.append(r > t)
    sums = np.concatenate(mats, axis=0).astype(np.float32)
    high_bit = np.floor(np.log2(np.maximum(t ^ r, 1))).astype(np.int32)
    level = np.where(t > r, high_bit, np.where(t == r, GLA_LEVELS, GLA_LEVELS + 1))
    return jnp.asarray(sums, BF16), jnp.asarray(level.astype(np.int32))


def _gla_kernel(zq_ref, zf_ref, zi_ref, zg_ref, lb_ref, gw_ref, sums_ref, level_ref,
                og_ref, sout_ref, st_ref, o_s, *, rows):
    c = GLA_CHUNK
    tb = pl.program_id(2)

    @pl.when(tb == 0)
    def _zero_state():
        st_ref[...] = jnp.zeros(st_ref.shape, F32)

    zq = zq_ref[...]
    q_all = zq * _sigmoid(zq)
    lb = lb_ref[...]
    fg = lb + (1.0 - lb) * _sigmoid(zf_ref[...])
    logf_all = jnp.log(fg) * LOG2E
    k_all = 1.0 - fg
    v_all = zi_ref[...]
    level = level_ref[...]
    row_id = lax.broadcasted_iota(jnp.int32, (c, HG_DK), 0)
    contract_last = (((1,), (1,)), ((), ()))

    st = st_ref[...]
    for ch in range(rows // c):
        sl = slice(ch * c, (ch + 1) * c)
        q = q_all[sl]
        k = k_all[sl]
        v = v_all[sl].astype(BF16)
        logf = logf_all[sl]
        hi = logf.astype(BF16)
        lo = (logf - hi.astype(F32)).astype(BF16)
        w = jnp.dot(sums_ref[...], jnp.concatenate([hi, lo], axis=1),
                    preferred_element_type=F32)
        f = jnp.exp2(w[:, 0:HG_DK] + w[:, HG_DK:2 * HG_DK])
        a = jnp.zeros((c, c), F32)
        for lev in range(GLA_LEVELS):
            upper = ((row_id >> lev) & 1) == 1
            y = (jnp.where(upper, q, k) * f[lev * c:(lev + 1) * c]).astype(BF16)
            a = jnp.where(level == lev,
                          lax.dot_general(y, y, contract_last,
                                          preferred_element_type=F32), a)
        a = jnp.where(level == GLA_LEVELS,
                      lax.dot_general(q.astype(BF16), k.astype(BF16), contract_last,
                                      preferred_element_type=F32), a)
        o_intra = jnp.dot(a.astype(BF16), v, preferred_element_type=F32)
        qh = (q * f[GLA_LEVELS * c:(GLA_LEVELS + 1) * c]).astype(BF16)
        kh = (k * f[(GLA_LEVELS + 1) * c:(GLA_LEVELS + 2) * c]).astype(BF16)
        o_inter = lax.dot_general(qh, st.astype(BF16), contract_last,
                                  preferred_element_type=F32)
        decay = f[(GLA_LEVELS + 1) * c - 1:(GLA_LEVELS + 1) * c]
        st = st * decay + lax.dot_general(v, kh, (((0,), (0,)), ((), ())),
                                          preferred_element_type=F32)
        o_s[sl, :] = o_inter + o_intra
    st_ref[...] = st
    zg = zg_ref[...]
    og_ref[...] = _rms(o_s[...], gw_ref[...]) * (zg * _sigmoid(zg))

    @pl.when(tb == pl.num_programs(2) - 1)
    def _store_state():
        sout_ref[0, 0] = st.T


def _gla(z, lb, gw, nseq, t, rows):
    nt = t // rows
    sums, level = _gla_tables()

    def col_spec(base):
        return pl.BlockSpec((rows, HG_DK),
                            lambda s, h, tb: (s * nt + tb, base + h))

    head_vec = pl.BlockSpec((1, HG_DK), lambda s, h, tb: (0, h))
    kern = functools.partial(_gla_kernel, rows=rows)
    return pl.pallas_call(
        kern,
        grid=(nseq, HG_HEADS, nt),
        in_specs=[col_spec(0), col_spec(HG_HEADS), col_spec(2 * HG_HEADS),
                  col_spec(3 * HG_HEADS), head_vec,
                  pl.BlockSpec((1, HG_DV), lambda s, h, tb: (0, 0)),
                  _const_spec(sums.shape), _const_spec(level.shape)],
        out_specs=[pl.BlockSpec((rows, HG_DV), lambda s, h, tb: (s * nt + tb, h)),
                   pl.BlockSpec((1, 1, HG_DK, HG_DV), lambda s, h, tb: (s, h, 0, 0))],
        out_shape=[jax.ShapeDtypeStruct((nseq * t, D_MODEL), F32),
                   jax.ShapeDtypeStruct((nseq, HG_HEADS, HG_DK, HG_DV), F32)],
        scratch_shapes=[pltpu.VMEM((HG_DV, HG_DK), F32),
                        pltpu.VMEM((rows, HG_DV), F32)],
        compiler_params=_cparams(("parallel", "parallel", "arbitrary")),
        name="hgrn2_gla",
    )(z, z, z, z, lb, gw, sums, level)


def _gla_decode_kernel(z_ref, lb_ref, gw_ref, s0_ref, og_ref, sout_ref):
    z = z_ref[0]
    eye = (lax.broadcasted_iota(jnp.int32, (HG_DK, HG_DK), 0)
           == lax.broadcasted_iota(jnp.int32, (HG_DK, HG_DK), 1))

    def column(row_vec):
        return jnp.sum(jnp.where(eye, row_vec, 0.0), axis=-1, keepdims=True)

    gw = gw_ref[...]
    for h in range(HG_HEADS):
        cols = slice(h * HG_DK, (h + 1) * HG_DK)
        zq = z[:, cols]
        q = zq * _sigmoid(zq)
        lb = lb_ref[:, cols]
        fg = lb + (1.0 - lb) * _sigmoid(z[:, HG_FDIM + h * HG_DK:HG_FDIM + (h + 1) * HG_DK])
        kk = 1.0 - fg
        v = z[:, 2 * HG_FDIM + h * HG_DV:2 * HG_FDIM + (h + 1) * HG_DV]
        zg = z[:, 2 * HG_FDIM + D_MODEL + h * HG_DV:2 * HG_FDIM + D_MODEL + (h + 1) * HG_DV]
        s_new = column(fg) * s0_ref[0, h] + column(kk) * v
        sout_ref[0, h] = s_new
        o = jnp.sum(column(q) * s_new, axis=0, keepdims=True)
        og_ref[0, :, cols] = _rms(o, gw) * (zg * _sigmoid(zg))


def _gla_decode(z, lb, gw, s0):
    nseq = z.shape[0]
    tok = lambda s: (s, 0, 0)
    state = pl.BlockSpec((1, HG_HEADS, HG_DK, HG_DV), lambda s: (s, 0, 0, 0))
    return pl.pallas_call(
        _gla_decode_kernel,
        grid=(nseq,),
        in_specs=[pl.BlockSpec((1, 1, ODD_IN), tok),
                  _const_spec((1, HG_FDIM)),
                  _const_spec((1, HG_DV)),
                  state],
        out_specs=[pl.BlockSpec((1, 1, D_MODEL), tok), state],
        out_shape=[jax.ShapeDtypeStruct((nseq, 1, D_MODEL), F32),
                   jax.ShapeDtypeStruct(s0.shape, F32)],
        compiler_params=_cparams(("parallel",)),
        name="hgrn2_decode",
    )(z.reshape(nseq, 1, ODD_IN), lb, gw, s0)


def _odd_out_kernel(a_ref, w_ref, x_ref, nw_ref, out_ref):
    y = jnp.dot(a_ref[...].astype(BF16), w_ref[...], preferred_element_type=F32)
    out_ref[...] = x_ref[...] + _rms(y, nw_ref[...])


def _odd_out(a, w, x, nw, tm):
    n = x.shape[0]
    row = lambda i: (i, 0)
    return pl.pallas_call(
        _odd_out_kernel,
        grid=(n // tm,),
        in_specs=[pl.BlockSpec((tm, D_MODEL), row),
                  _const_spec((D_MODEL, D_MODEL)),
                  pl.BlockSpec((tm, D_MODEL), row),
                  _const_spec((1, D_MODEL))],
        out_specs=pl.BlockSpec((tm, D_MODEL), row),
        out_shape=jax.ShapeDtypeStruct((n, D_MODEL), F32),
        compiler_params=_cparams(("parallel",)),
        name="odd_out",
    )(a, w, x, nw)


def _rope_tables(pos):
    half = HEAD_DIM // 2
    inv = ROPE_THETA ** (-jnp.arange(half, dtype=F32) / half)
    ang = pos.astype(F32)[:, None] * inv[None, :]
    cos = jnp.tile(jnp.cos(ang), (1, LANES // half))
    sin = jnp.tile(jnp.concatenate([-jnp.sin(ang), jnp.sin(ang)], axis=1),
                   (1, LANES // HEAD_DIM))
    return cos, sin


def kernel(x_prompt, x_sample, cache_k, cache_v, state_conv, state_hgrn, page_table,
           norm_w, w_in_even, w_out_even, conv_w, lambda_q1, lambda_k1, lambda_q2,
           lambda_k2, subln_w, w_in_odd, w_out_odd, hgrn_lb, gnorm_w, w_ffn_in,
           w_ffn_out):
    nb, t, _ = x_prompt.shape
    nd, td, _ = x_sample.shape
    assert td == 1
    depth = norm_w.shape[0]
    n_pages = page_table.shape[1]
    past_len = n_pages * PAGE_SIZE
    tm_p = 512
    tm_d = nd
    dec_pages = 8
    gla_rows = 512

    cos_p, sin_p = _rope_tables(jnp.tile(jnp.arange(t), nb))
    cos_d, sin_d = _rope_tables(jnp.full((nd,), past_len))
    p_lb = jax.nn.softmax(hgrn_lb.astype(F32), axis=0)
    lower_bounds = jnp.cumsum(p_lb, axis=0) - p_lb[0]

    n_even, n_phys = cache_k.shape[:2]
    ck = jnp.transpose(cache_k, (0, 1, 3, 4, 5, 2)).reshape(n_even, n_phys, A_QK, PAGE_SIZE)
    cv = cache_v.reshape(n_even, n_phys, PAGE_SIZE * A_HEADS, A_VDIM)
    nw = norm_w.reshape(depth, 4, 1, D_MODEL)

    xp = x_prompt.reshape(nb * t, D_MODEL)
    xs = x_sample.reshape(nd, D_MODEL)
    kp_l, vp_l, cp_l, hp_l = [], [], [], []
    ks_l, vs_l, cs_l, hs_l = [], [], [], []
    for layer in range(depth):
        if layer % 2 == 0:
            e = layer // 2
            lam_init = 0.8 - 0.6 * math.exp(-0.3 * layer)
            lam = (jnp.exp(jnp.sum(lambda_q1[e].astype(F32) * lambda_k1[e].astype(F32)))
                   - jnp.exp(jnp.sum(lambda_q2[e].astype(F32) * lambda_k2[e].astype(F32)))
                   + lam_init).reshape(1)
            w_in = w_in_even[e].astype(BF16)
            w_out = w_out_even[e].astype(BF16)
            sw = subln_w[e].reshape(1, A_VDIM)
            cw = conv_w[e]

            q, k, v, bg, u = _even_in(xp, nw[layer, 0], w_in, cos_p, sin_p, tm_p)
            o = _flash(lam, q, k, v, sw, nb, t, 1.0 - lam_init, tm_p)
            xp_mid = _even_out(o, bg, u, None, cw, w_out, xp, nw[layer, 1], tm_p, t)

            qd, kd, vd, bgd, ud = _even_in(xs, nw[layer, 0], w_in, cos_d, sin_d, tm_d)
            od = _decode_attn(page_table, lam, qd.reshape(nd, 1, A_QK),
                              kd.reshape(nd, 1, A_QK), vd.reshape(nd, 1, A_V), sw,
                              ck, cv, e, 1.0 - lam_init, dec_pages)
            sc = state_conv[e]
            xs_mid = _even_out(od.reshape(nd, A_V), bgd, ud, (sc[:, 1], sc[:, 0]), cw,
                               w_out, xs, nw[layer, 1], tm_d, td)

            kp_l.append(k.reshape(nb, t, A_HEADS, 2, HEAD_DIM))
            vp_l.append(v.reshape(nb, t, A_HEADS, A_VDIM))
            cp_l.append(u.reshape(nb, t, B_WIDTH)[:, t - (CONV_W - 1):])
            ks_l.append(kd.reshape(nd, td, A_HEADS, 2, HEAD_DIM))
            vs_l.append(vd.reshape(nd, td, A_HEADS, A_VDIM))
            cs_l.append(jnp.concatenate([sc[:, 1:], ud[:, None, :]], axis=1))
        else:
            o_idx = layer // 2
            w_in = w_in_odd[o_idx].astype(BF16)
            w_out = w_out_odd[o_idx].astype(BF16)
            lb = lower_bounds[layer].reshape(1, HG_FDIM)
            gw = gnorm_w[o_idx].reshape(1, HG_DV)

            z = _odd_in(xp, nw[layer, 0], w_in, tm_p)
            og, sp = _gla(z, lb, gw, nb, t, gla_rows)
            xp_mid = _odd_out(og, w_out, xp, nw[layer, 1], tm_p)

            zd = _odd_in(xs, nw[layer, 0], w_in, tm_d)
            ogd, ss = _gla_decode(zd, lb, gw, state_hgrn[o_idx])
            xs_mid = _odd_out(ogd.reshape(nd, D_MODEL), w_out, xs, nw[layer, 1], tm_d)
            hp_l.append(sp)
            hs_l.append(ss)

        wi = w_ffn_in[layer].astype(BF16)
        wo = w_ffn_out[layer].astype(BF16)
        xp = _ffn(xp_mid, nw[layer, 2], wi, wo, nw[layer, 3], tm_p)
        xs = _ffn(xs_mid, nw[layer, 2], wi, wo, nw[layer, 3], tm_d)

    return (xp.reshape(nb, t, D_MODEL), xs.reshape(nd, td, D_MODEL),
            jnp.stack(kp_l), jnp.stack(vp_l), jnp.stack(cp_l), jnp.stack(hp_l),
            jnp.stack(ks_l), jnp.stack(vs_l), jnp.stack(cs_l), jnp.stack(hs_l))
```

```python
import functools
import math

import numpy as np
import jax
import jax.numpy as jnp
from jax import lax
from jax.experimental import pallas as pl
from jax.experimental.pallas import tpu as pltpu

D_MODEL = 1024
PAGE_SIZE = 128
HEAD_DIM = 64
A_HEADS = 4
A_QK = A_HEADS * 2 * HEAD_DIM
A_VDIM = 2 * HEAD_DIM
A_V = A_HEADS * A_VDIM
ATTN_SCALE = HEAD_DIM ** -0.5
ROPE_THETA = 10000.0
B_WIDTH = D_MODEL // 2
CONV_W = 3
EVEN_IN = 2 * A_QK + A_V + 3 * B_WIDTH
HG_HEADS = 8
HG_DK = 128
HG_FDIM = HG_HEADS * HG_DK
HG_DV = D_MODEL // HG_HEADS
ODD_IN = 2 * HG_FDIM + 2 * D_MODEL
D_FF = -(-8 * D_MODEL // (3 * 256)) * 256
EPS = 1e-6
LOG2E = math.log2(math.e)

LANES = 128
SUBLANES = 8
VMEM_LIMIT_BYTES = 56 * 1024 * 1024
GLA_CHUNK = 128
GLA_LEVELS = 7
GLA_SUM_LEVELS = 3
FFN_CHUNK = 256
FLASH_ROWS = 128

F32 = jnp.float32
BF16 = jnp.bfloat16


def _cparams(semantics):
    return pltpu.CompilerParams(dimension_semantics=semantics,
                                vmem_limit_bytes=VMEM_LIMIT_BYTES)


def _rms(x, w):
    return x * lax.rsqrt(jnp.mean(x * x, axis=-1, keepdims=True) + EPS) * w


def _sigmoid(x):
    return 1.0 / (1.0 + jnp.exp(-x))


def _tile_lanes(x, n):
    return x if n == 1 else jnp.concatenate([x] * n, axis=1)


def _const_spec(shape):
    nd = len(shape)
    return pl.BlockSpec(shape, lambda *_: (0,) * nd, pipeline_mode=pl.Buffered(1))


def _even_in_kernel(x_ref, nw_ref, w_ref, cos_ref, sin_ref,
                    q_ref, k_ref, v_ref, bg_ref, u_ref):
    xn = _rms(x_ref[...], nw_ref[...]).astype(BF16)
    cos = cos_ref[...]
    sin = sin_ref[...]
    lane = lax.broadcasted_iota(jnp.int32, cos.shape, 1)
    first_half = (lane % HEAD_DIM) < (HEAD_DIM // 2)

    def proj(c):
        return jnp.dot(xn, w_ref[:, c * A_QK:(c + 1) * A_QK],
                       preferred_element_type=F32)

    def rope(z, out_ref):
        for s in range(A_QK // LANES):
            zs = z[:, s * LANES:(s + 1) * LANES]
            swapped = jnp.where(first_half,
                                pltpu.roll(zs, LANES - HEAD_DIM // 2, 1),
                                pltpu.roll(zs, HEAD_DIM // 2, 1))
            out_ref[:, s * LANES:(s + 1) * LANES] = zs * cos + swapped * sin

    rope(proj(0), q_ref)
    rope(proj(1), k_ref)
    v_ref[...] = proj(2)
    bg_ref[...] = proj(3)
    u_ref[...] = proj(4) * proj(5)


def _even_in(x, nw, w, cos, sin, tm):
    n = x.shape[0]
    row = lambda i: (i, 0)
    pos_tiles = cos.shape[0] // tm
    pos = lambda i: (i % pos_tiles, 0)
    out = jax.ShapeDtypeStruct((n, A_QK), F32)
    return pl.pallas_call(
        _even_in_kernel,
        grid=(n // tm,),
        in_specs=[pl.BlockSpec((tm, D_MODEL), row),
                  _const_spec((1, D_MODEL)),
                  _const_spec((D_MODEL, EVEN_IN)),
                  pl.BlockSpec((tm, LANES), pos),
                  pl.BlockSpec((tm, LANES), pos)],
        out_specs=[pl.BlockSpec((tm, A_QK), row)] * 5,
        out_shape=[out] * 5,
        compiler_params=_cparams(("parallel",)),
        name="even_in",
    )(x, nw, w, cos, sin)


def _flash_kernel(ii_ref, jj_ref, lam_ref, q_ref, k_ref, v_ref, sw_ref, o_ref,
                  qs_ref, m_ref, acc_ref, *, tq, lam_scale):
    pair = pl.program_id(2)
    i = ii_ref[pair]
    j = jj_ref[pair]

    @pl.when(j == 0)
    def _init():
        qs_ref[...] = (q_ref[...] * (ATTN_SCALE * LOG2E)).astype(BF16)
        m_ref[...] = jnp.full(m_ref.shape, -jnp.inf, F32)
        acc_ref[...] = jnp.zeros(acc_ref.shape, F32)

    def step(diagonal):
        kf = k_ref[...]
        lane = lax.broadcasted_iota(jnp.int32, kf.shape, 1)
        v = v_ref[...].astype(BF16)
        v_ext = jnp.concatenate([v, jnp.ones_like(v)], axis=1)
        tri = (lax.broadcasted_iota(jnp.int32, (FLASH_ROWS, FLASH_ROWS), 1)
               <= lax.broadcasted_iota(jnp.int32, (FLASH_ROWS, FLASH_ROWS), 0))
        for mp in range(2):
            keep = (lane < HEAD_DIM) if mp == 0 else (lane >= HEAD_DIM)
            k = jnp.where(keep, kf, 0.0).astype(BF16)
            for rc in range(tq // FLASH_ROWS):
                rows = slice(rc * FLASH_ROWS, (rc + 1) * FLASH_ROWS)
                first = diagonal and (rc + 1) * FLASH_ROWS <= tq // 2
                ncol = tq // 2 if first else tq
                s = lax.dot_general(qs_ref[rows, :], k[0:ncol], (((1,), (1,)), ((), ())),
                                    preferred_element_type=F32)
                if diagonal:
                    lo = rc * FLASH_ROWS
                    hi = lo + FLASH_ROWS
                    parts = [s[:, 0:lo]] if rc > 0 else []
                    parts.append(jnp.where(tri, s[:, lo:hi], -jnp.inf))
                    if hi < ncol:
                        parts.append(jnp.full((FLASH_ROWS, ncol - hi), -jnp.inf, F32))
                    s = jnp.concatenate(parts, axis=1)
                m_prev = m_ref[mp, rows, :]
                m_new = jnp.maximum(m_prev, jnp.max(s, axis=-1, keepdims=True))
                alpha = jnp.exp2(m_prev - m_new)
                p = jnp.exp2(s - _tile_lanes(m_new, ncol // LANES))
                pv = jnp.dot(p.astype(BF16), v_ext[0:ncol], preferred_element_type=F32)
                acc_ref[mp, rows, :] = _tile_lanes(alpha, 2) * acc_ref[mp, rows, :] + pv
                m_ref[mp, rows, :] = m_new

    @pl.when(j < i)
    def _off_diag():
        step(False)

    @pl.when(j == i)
    def _diag():
        step(True)
        a0 = acc_ref[0]
        a1 = acc_ref[1]
        o = (a0[:, 0:LANES] / a0[:, LANES:2 * LANES]
             - lam_ref[0] * (a1[:, 0:LANES] / a1[:, LANES:2 * LANES]))
        o_ref[...] = _rms(o, sw_ref[...]) * lam_scale


def _flash(lam, q, k, v, sw, nb, t, lam_scale, tq):
    nq = t // tq
    pairs = [(i, j) for i in range(nq) for j in range(i + 1)]
    ii = jnp.asarray(np.array([p[0] for p in pairs], np.int32))
    jj = jnp.asarray(np.array([p[1] for p in pairs], np.int32))
    kern = functools.partial(_flash_kernel, tq=tq, lam_scale=lam_scale)
    qmap = lambda b, h, p, ii, jj: (b * nq + ii[p], h)
    kmap = lambda b, h, p, ii, jj: (b * nq + jj[p], h)
    grid_spec = pltpu.PrefetchScalarGridSpec(
        num_scalar_prefetch=2,
        grid=(nb, A_HEADS, len(pairs)),
        in_specs=[pl.BlockSpec(memory_space=pltpu.SMEM),
                  pl.BlockSpec((tq, LANES), qmap),
                  pl.BlockSpec((tq, LANES), kmap),
                  pl.BlockSpec((tq, LANES), kmap),
                  pl.BlockSpec((1, LANES), lambda b, h, p, ii, jj: (0, 0))],
        out_specs=pl.BlockSpec((tq, LANES), qmap),
        scratch_shapes=[pltpu.VMEM((tq, LANES), BF16),
                        pltpu.VMEM((2, tq, LANES), F32),
                        pltpu.VMEM((2, tq, 2 * LANES), F32)])
    return pl.pallas_call(
        kern,
        grid_spec=grid_spec,
        out_shape=jax.ShapeDtypeStruct((nb * t, A_V), F32),
        compiler_params=_cparams(("parallel", "parallel", "arbitrary")),
        name="flash_diff_attn",
    )(ii, jj, lam, q, k, v, sw)


DEC_ROWS = 16


def _decode_attn_kernel(pt_ref, lam_ref, q_ref, kc_ref, vc_ref, sw_ref, *refs,
                        pages, lam_scale):
    del pt_ref
    kp = refs[0:pages]
    vp = refs[pages:2 * pages]
    o_ref = refs[2 * pages]
    qbd_ref, m_ref, l_ref, acc_ref = refs[2 * pages + 1:]
    j = pl.program_id(1)

    @pl.when(j == 0)
    def _init():
        q = jnp.broadcast_to(q_ref[0] * (ATTN_SCALE * LOG2E), (DEC_ROWS, A_QK))
        row = lax.broadcasted_iota(jnp.int32, q.shape, 0)
        blk = lax.broadcasted_iota(jnp.int32, q.shape, 1) // HEAD_DIM
        qbd_ref[...] = jnp.where(row == blk, q, 0.0)
        m_ref[...] = jnp.full(m_ref.shape, -jnp.inf, F32)
        l_ref[...] = jnp.zeros(l_ref.shape, F32)
        acc_ref[...] = jnp.zeros(acc_ref.shape, F32)

    qbd = qbd_ref[...].astype(BF16)
    s = jnp.concatenate(
        [jnp.dot(qbd, kp[p][...].astype(BF16), preferred_element_type=F32)
         for p in range(pages)], axis=1)
    m_prev = m_ref[...]
    m_new = jnp.maximum(m_prev, jnp.max(s, axis=-1, keepdims=True))
    alpha = jnp.exp2(m_prev - m_new)
    p_all = jnp.exp2(s - _tile_lanes(m_new, pages))
    l_ref[...] = alpha * l_ref[...] + jnp.sum(p_all, axis=-1, keepdims=True)
    p_bf = p_all.astype(BF16)
    for h in range(A_HEADS):
        pv = jnp.zeros((DEC_ROWS, A_VDIM), F32)
        for p in range(pages):
            v_h = vp[p][pl.ds(h, PAGE_SIZE, stride=A_HEADS), :]
            pv = pv + jnp.dot(p_bf[:, p * PAGE_SIZE:(p + 1) * PAGE_SIZE],
                              v_h.astype(BF16), preferred_element_type=F32)
        acc_ref[h] = alpha * acc_ref[h] + pv
    m_ref[...] = m_new

    @pl.when(j == pl.num_programs(1) - 1)
    def _finish():
        s_cur = jnp.sum(qbd_ref[...] * kc_ref[0], axis=-1, keepdims=True)
        m_old = m_ref[...]
        m_fin = jnp.maximum(m_old, s_cur)
        a_fin = jnp.exp2(m_old - m_fin)
        p_cur = jnp.exp2(s_cur - m_fin)
        l_fin = a_fin * l_ref[...] + p_cur
        lam = lam_ref[0]
        sw = sw_ref[...]
        vc = vc_ref[0]
        for h in range(A_HEADS):
            cols = slice(h * A_VDIM, (h + 1) * A_VDIM)
            a = (a_fin * acc_ref[h] + p_cur * vc[:, cols]) / l_fin
            o = a[2 * h:2 * h + 1, :] - lam * a[2 * h + 1:2 * h + 2, :]
            o_ref[0, :, cols] = _rms(o, sw) * lam_scale


def _decode_attn(page_table, lam, qd, kd, vd, sw, cache_kt, cache_vf, e, lam_scale,
                 pages):
    nb, n_pages = page_table.shape
    kern = functools.partial(_decode_attn_kernel, pages=pages, lam_scale=lam_scale)
    tok = lambda b, j, pt: (b, 0, 0)

    def page_spec(p):
        return pl.BlockSpec((None, None, A_QK, PAGE_SIZE),
                            lambda b, j, pt: (e, pt[b, j * pages + p], 0, 0))

    grid_spec = pltpu.PrefetchScalarGridSpec(
        num_scalar_prefetch=1,
        grid=(nb, n_pages // pages),
        in_specs=([pl.BlockSpec(memory_space=pltpu.SMEM),
                   pl.BlockSpec((1, 1, A_QK), tok),
                   pl.BlockSpec((1, 1, A_QK), tok),
                   pl.BlockSpec((1, 1, A_V), tok),
                   pl.BlockSpec((1, LANES), lambda b, j, pt: (0, 0))]
                  + [page_spec(p) for p in range(pages)] * 2),
        out_specs=pl.BlockSpec((1, 1, A_V), tok),
        scratch_shapes=[pltpu.VMEM((DEC_ROWS, A_QK), F32),
                        pltpu.VMEM((DEC_ROWS, LANES), F32),
                        pltpu.VMEM((DEC_ROWS, LANES), F32),
                        pltpu.VMEM((A_HEADS, DEC_ROWS, A_VDIM), F32)])
    return pl.pallas_call(
        kern,
        grid_spec=grid_spec,
        out_shape=jax.ShapeDtypeStruct((nb, 1, A_V), F32),
        compiler_params=_cparams(("parallel", "arbitrary")),
        name="decode_diff_attn",
    )(page_table, lam, qd, kd, vd, sw, *([cache_kt] * pages), *([cache_vf] * pages))


def _even_out_kernel(*refs, tm, seq_tiles, explicit_prev):
    if explicit_prev:
        (o_ref, bg_ref, u_ref, um1_ref, um2_ref, cw_ref, w_ref, x_ref, nw_ref,
         out_ref) = refs
        u = u_ref[...]
        um1 = um1_ref[...]
        um2 = um2_ref[...]
    else:
        o_ref, bg_ref, u_ref, halo_ref, cw_ref, w_ref, x_ref, nw_ref, out_ref = refs
        u = u_ref[...]
        halo = halo_ref[...]
        halo = jnp.where(pl.program_id(0) % seq_tiles == 0, 0.0, halo)
        row = lax.broadcasted_iota(jnp.int32, u.shape, 0)
        h1 = halo[SUBLANES - 1:SUBLANES, :]
        h2 = halo[SUBLANES - 2:SUBLANES - 1, :]
        um1 = jnp.where(row == 0, h1, pltpu.roll(u, 1, 0))
        um2 = jnp.where(row == 0, h2, jnp.where(row == 1, h1, pltpu.roll(u, 2, 0)))
    cw = cw_ref[...]
    conv = um2 * cw[0:1, :] + um1 * cw[1:2, :] + u * cw[2:3, :]
    yb = bg_ref[...] * conv
    y = (jnp.dot(o_ref[...].astype(BF16), w_ref[0:A_V, :], preferred_element_type=F32)
         + jnp.dot(yb.astype(BF16), w_ref[A_V:A_V + B_WIDTH, :],
                   preferred_element_type=F32))
    out_ref[...] = x_ref[...] + _rms(y, nw_ref[...])


def _even_out(o, bg, u, prev, cw, w, x, nw, tm, seq_len):
    n = x.shape[0]
    row = lambda i: (i, 0)
    half = pl.BlockSpec((tm, B_WIDTH), row)
    explicit_prev = isinstance(prev, tuple)
    if explicit_prev:
        prev_args = list(prev)
        prev_specs = [half, half]
    else:
        prev_args = [u]
        prev_specs = [pl.BlockSpec(
            (SUBLANES, B_WIDTH),
            lambda i: (jnp.maximum(i * (tm // SUBLANES) - 1, 0), 0))]
    kern = functools.partial(_even_out_kernel, tm=tm, seq_tiles=seq_len // tm,
                             explicit_prev=explicit_prev)
    return pl.pallas_call(
        kern,
        grid=(n // tm,),
        in_specs=([half, half, half] + prev_specs
                  + [_const_spec((CONV_W, B_WIDTH)),
                     _const_spec((D_MODEL, D_MODEL)),
                     pl.BlockSpec((tm, D_MODEL), row),
                     _const_spec((1, D_MODEL))]),
        out_specs=pl.BlockSpec((tm, D_MODEL), row),
        out_shape=jax.ShapeDtypeStruct((n, D_MODEL), F32),
        compiler_params=_cparams(("parallel",)),
        name="even_out",
    )(o, bg, u, *prev_args, cw, w, x, nw)


def _ffn_kernel(x_ref, nw_in_ref, wi_ref, wo_ref, nw_out_ref, out_ref):
    x = x_ref[...]
    xn = _rms(x, nw_in_ref[...]).astype(BF16)
    y = jnp.zeros(x.shape, F32)
    for c in range(D_FF // FFN_CHUNK):
        lo = c * FFN_CHUNK
        gate = jnp.dot(xn, wi_ref[:, lo:lo + FFN_CHUNK], preferred_element_type=F32)
        up = jnp.dot(xn, wi_ref[:, D_FF + lo:D_FF + lo + FFN_CHUNK],
                     preferred_element_type=F32)
        act = (gate * _sigmoid(gate) * up).astype(BF16)
        y = y + jnp.dot(act, wo_ref[lo:lo + FFN_CHUNK, :], preferred_element_type=F32)
    out_ref[...] = x + _rms(y, nw_out_ref[...])


def _ffn(x, nw_in, wi, wo, nw_out, tm):
    n = x.shape[0]
    row = lambda i: (i, 0)
    return pl.pallas_call(
        _ffn_kernel,
        grid=(n // tm,),
        in_specs=[pl.BlockSpec((tm, D_MODEL), row),
                  _const_spec((1, D_MODEL)),
                  _const_spec((D_MODEL, 2 * D_FF)),
                  _const_spec((D_FF, D_MODEL)),
                  _const_spec((1, D_MODEL))],
        out_specs=pl.BlockSpec((tm, D_MODEL), row),
        out_shape=jax.ShapeDtypeStruct((n, D_MODEL), F32),
        compiler_params=_cparams(("parallel",)),
        name="ffn",
    )(x, nw_in, wi, wo, nw_out)


def _odd_in_kernel(x_ref, nw_ref, w_ref, z_ref):
    xn = _rms(x_ref[...], nw_ref[...]).astype(BF16)
    for c in range(ODD_IN // D_MODEL):
        cols = slice(c * D_MODEL, (c + 1) * D_MODEL)
        z_ref[:, cols] = jnp.dot(xn, w_ref[:, cols], preferred_element_type=F32)


def _odd_in(x, nw, w, tm):
    n = x.shape[0]
    row = lambda i: (i, 0)
    return pl.pallas_call(
        _odd_in_kernel,
        grid=(n // tm,),
        in_specs=[pl.BlockSpec((tm, D_MODEL), row),
                  _const_spec((1, D_MODEL)),
                  _const_spec((D_MODEL, ODD_IN))],
        out_specs=pl.BlockSpec((tm, ODD_IN), row),
        out_shape=jax.ShapeDtypeStruct((n, ODD_IN), F32),
        compiler_params=_cparams(("parallel",)),
        name="odd_in",
    )(x, nw, w)


def _gla_tables():
    c = GLA_CHUNK
    t = np.arange(c)[:, None]
    r = np.arange(c)[None, :]
    mats = []
    for lev in range(GLA_SUM_LEVELS):
        half = 1 << lev
        mid = (t // (2 * half)) * (2 * half) + half
        mats.append(np.where(t >= mid, (r >= mid) & (r <= t), (r > t) & (r < mid)))
    mats.append(r <= t)
    sums = np.concatenate(mats, axis=0).astype(np.float32)
    high_bit = np.floor(np.log2(np.maximum(t ^ r, 1))).astype(np.int32)
    level = np.where(t > r, high_bit, np.where(t == r, GLA_LEVELS, GLA_LEVELS + 1))
    return jnp.asarray(sums, BF16), jnp.asarray(level.astype(np.int32))


def _gla_kernel(zq_ref, zf_ref, zi_ref, zg_ref, lb_ref, gw_ref, sums_ref, level_ref,
                og_ref, sout_ref, st_ref, o_s, *, rows):
    c = GLA_CHUNK
    tb = pl.program_id(2)

    @pl.when(tb == 0)
    def _zero_state():
        st_ref[...] = jnp.zeros(st_ref.shape, F32)

    zq = zq_ref[...]
    q_all = zq * _sigmoid(zq)
    lb = lb_ref[...]
    fg = lb + (1.0 - lb) * _sigmoid(zf_ref[...])
    logf_all = jnp.log(fg) * LOG2E
    k_all = 1.0 - fg
    v_all = zi_ref[...]
    level = level_ref[...]
    row_id = lax.broadcasted_iota(jnp.int32, (c, HG_DK), 0)
    contract_last = (((1,), (1,)), ((), ()))

    st = st_ref[...]
    for ch in range(rows // c):
        sl = slice(ch * c, (ch + 1) * c)
        q = q_all[sl]
        k = k_all[sl]
        v = v_all[sl].astype(BF16)
        logf = logf_all[sl]
        hi = logf.astype(BF16)
        lo = (logf - hi.astype(F32)).astype(BF16)
        w = jnp.dot(sums_ref[...], jnp.concatenate([hi, lo], axis=1),
                    preferred_element_type=F32)
        w = w[:, 0:HG_DK] + w[:, HG_DK:2 * HG_DK]
        cum = w[GLA_SUM_LEVELS * c:(GLA_SUM_LEVELS + 1) * c]
        a = jnp.zeros((c, c), F32)
        for lev in range(GLA_LEVELS):
            if lev < GLA_SUM_LEVELS:
                expo = w[lev * c:(lev + 1) * c]
            else:
                half = 1 << lev
                at_split = jnp.concatenate(
                    [jnp.broadcast_to(cum[b0 + half - 1:b0 + half, :], (2 * half, HG_DK))
                     for b0 in range(0, c, 2 * half)], axis=0)
                diff = cum - at_split
                expo = jnp.minimum(diff, -diff)
            upper = ((row_id >> lev) & 1) == 1
            y = (jnp.where(upper, q, k) * jnp.exp2(expo)).astype(BF16)
            a = jnp.where(level == lev,
                          lax.dot_general(y, y, contract_last,
                                          preferred_element_type=F32), a)
        a = jnp.where(level == GLA_LEVELS,
                      lax.dot_general(q.astype(BF16), k.astype(BF16), contract_last,
                                      preferred_element_type=F32), a)
        o_intra = jnp.dot(a.astype(BF16), v, preferred_element_type=F32)
        total = cum[c - 1:c, :]
        qh = (q * jnp.exp2(cum)).astype(BF16)
        kh = (k * jnp.exp2(total - cum)).astype(BF16)
        o_inter = lax.dot_general(qh, st.astype(BF16), contract_last,
                                  preferred_element_type=F32)
        st = st * jnp.exp2(total) + lax.dot_general(v, kh, (((0,), (0,)), ((), ())),
                                                    preferred_element_type=F32)
        o_s[sl, :] = o_inter + o_intra
    st_ref[...] = st
    zg = zg_ref[...]
    og_ref[...] = _rms(o_s[...], gw_ref[...]) * (zg * _sigmoid(zg))

    @pl.when(tb == pl.num_programs(2) - 1)
    def _store_state():
        sout_ref[0, 0] = st.T


def _gla(z, lb, gw, nseq, t, rows):
    nt = t // rows
    sums, level = _gla_tables()

    def col_spec(base):
        return pl.BlockSpec((rows, HG_DK),
                            lambda s, h, tb: (s * nt + tb, base + h))

    head_vec = pl.BlockSpec((1, HG_DK), lambda s, h, tb: (0, h))
    kern = functools.partial(_gla_kernel, rows=rows)
    return pl.pallas_call(
        kern,
        grid=(nseq, HG_HEADS, nt),
        in_specs=[col_spec(0), col_spec(HG_HEADS), col_spec(2 * HG_HEADS),
                  col_spec(3 * HG_HEADS), head_vec,
                  pl.BlockSpec((1, HG_DV), lambda s, h, tb: (0, 0)),
                  _const_spec(sums.shape), _const_spec(level.shape)],
        out_specs=[pl.BlockSpec((rows, HG_DV), lambda s, h, tb: (s * nt + tb, h)),
                   pl.BlockSpec((1, 1, HG_DK, HG_DV), lambda s, h, tb: (s, h, 0, 0))],
        out_shape=[jax.ShapeDtypeStruct((nseq * t, D_MODEL), F32),
                   jax.ShapeDtypeStruct((nseq, HG_HEADS, HG_DK, HG_DV), F32)],
        scratch_shapes=[pltpu.VMEM((HG_DV, HG_DK), F32),
                        pltpu.VMEM((rows, HG_DV), F32)],
        compiler_params=_cparams(("parallel", "parallel", "arbitrary")),
        name="hgrn2_gla",
    )(z, z, z, z, lb, gw, sums, level)


def _gla_decode_kernel(z_ref, lb_ref, gw_ref, s0_ref, og_ref, sout_ref):
    z = z_ref[0]
    eye = (lax.broadcasted_iota(jnp.int32, (HG_DK, HG_DK), 0)
           == lax.broadcasted_iota(jnp.int32, (HG_DK, HG_DK), 1))

    def column(row_vec):
        return jnp.sum(jnp.where(eye, row_vec, 0.0), axis=-1, keepdims=True)

    gw = gw_ref[...]
    for h in range(HG_HEADS):
        cols = slice(h * HG_DK, (h + 1) * HG_DK)
        zq = z[:, cols]
        q = zq * _sigmoid(zq)
        lb = lb_ref[:, cols]
        fg = lb + (1.0 - lb) * _sigmoid(z[:, HG_FDIM + h * HG_DK:HG_FDIM + (h + 1) * HG_DK])
        kk = 1.0 - fg
        v = z[:, 2 * HG_FDIM + h * HG_DV:2 * HG_FDIM + (h + 1) * HG_DV]
        zg = z[:, 2 * HG_FDIM + D_MODEL + h * HG_DV:2 * HG_FDIM + D_MODEL + (h + 1) * HG_DV]
        s_new = column(fg) * s0_ref[0, h] + column(kk) * v
        sout_ref[0, h] = s_new
        o = jnp.sum(column(q) * s_new, axis=0, keepdims=True)
        og_ref[0, :, cols] = _rms(o, gw) * (zg * _sigmoid(zg))


def _gla_decode(z, lb, gw, s0):
    nseq = z.shape[0]
    tok = lambda s: (s, 0, 0)
    state = pl.BlockSpec((1, HG_HEADS, HG_DK, HG_DV), lambda s: (s, 0, 0, 0))
    return pl.pallas_call(
        _gla_decode_kernel,
        grid=(nseq,),
        in_specs=[pl.BlockSpec((1, 1, ODD_IN), tok),
                  _const_spec((1, HG_FDIM)),
                  _const_spec((1, HG_DV)),
                  state],
        out_specs=[pl.BlockSpec((1, 1, D_MODEL), tok), state],
        out_shape=[jax.ShapeDtypeStruct((nseq, 1, D_MODEL), F32),
                   jax.ShapeDtypeStruct(s0.shape, F32)],
        compiler_params=_cparams(("parallel",)),
        name="hgrn2_decode",
    )(z.reshape(nseq, 1, ODD_IN), lb, gw, s0)


def _odd_out_kernel(a_ref, w_ref, x_ref, nw_ref, out_ref):
    y = jnp.dot(a_ref[...].astype(BF16), w_ref[...], preferred_element_type=F32)
    out_ref[...] = x_ref[...] + _rms(y, nw_ref[...])


def _odd_out(a, w, x, nw, tm):
    n = x.shape[0]
    row = lambda i: (i, 0)
    return pl.pallas_call(
        _odd_out_kernel,
        grid=(n // tm,),
        in_specs=[pl.BlockSpec((tm, D_MODEL), row),
                  _const_spec((D_MODEL, D_MODEL)),
                  pl.BlockSpec((tm, D_MODEL), row),
                  _const_spec((1, D_MODEL))],
        out_specs=pl.BlockSpec((tm, D_MODEL), row),
        out_shape=jax.ShapeDtypeStruct((n, D_MODEL), F32),
        compiler_params=_cparams(("parallel",)),
        name="odd_out",
    )(a, w, x, nw)


def _rope_tables(pos):
    half = HEAD_DIM // 2
    inv = ROPE_THETA ** (-jnp.arange(half, dtype=F32) / half)
    ang = pos.astype(F32)[:, None] * inv[None, :]
    cos = jnp.tile(jnp.cos(ang), (1, LANES // half))
    sin = jnp.tile(jnp.concatenate([-jnp.sin(ang), jnp.sin(ang)], axis=1),
                   (1, LANES // HEAD_DIM))
    return cos, sin


def kernel(x_prompt, x_sample, cache_k, cache_v, state_conv, state_hgrn, page_table,
           norm_w, w_in_even, w_out_even, conv_w, lambda_q1, lambda_k1, lambda_q2,
           lambda_k2, subln_w, w_in_odd, w_out_odd, hgrn_lb, gnorm_w, w_ffn_in,
           w_ffn_out):
    nb, t, _ = x_prompt.shape
    nd, td, _ = x_sample.shape
    assert td == 1
    depth = norm_w.shape[0]
    n_pages = page_table.shape[1]
    past_len = n_pages * PAGE_SIZE
    tm_p = 512
    tm_d = nd
    dec_pages = 8
    gla_rows = 512
    flash_tile = 1024

    cos_p, sin_p = _rope_tables(jnp.arange(t))
    cos_d, sin_d = _rope_tables(jnp.full((nd,), past_len))
    p_lb = jax.nn.softmax(hgrn_lb.astype(F32), axis=0)
    lower_bounds = jnp.cumsum(p_lb, axis=0) - p_lb[0]

    n_even, n_phys = cache_k.shape[:2]
    ck = jnp.transpose(cache_k, (0, 1, 3, 4, 5, 2)).reshape(n_even, n_phys, A_QK, PAGE_SIZE)
    cv = cache_v.reshape(n_even, n_phys, PAGE_SIZE * A_HEADS, A_VDIM)
    nw = norm_w.reshape(depth, 4, 1, D_MODEL)

    xp = x_prompt.reshape(nb * t, D_MODEL)
    xs = x_sample.reshape(nd, D_MODEL)
    kp_l, vp_l, cp_l, hp_l = [], [], [], []
    ks_l, vs_l, cs_l, hs_l = [], [], [], []
    for layer in range(depth):
        if layer % 2 == 0:
            e = layer // 2
            lam_init = 0.8 - 0.6 * math.exp(-0.3 * layer)
            lam = (jnp.exp(jnp.sum(lambda_q1[e].astype(F32) * lambda_k1[e].astype(F32)))
                   - jnp.exp(jnp.sum(lambda_q2[e].astype(F32) * lambda_k2[e].astype(F32)))
                   + lam_init).reshape(1)
            w_in = w_in_even[e].astype(BF16)
            w_out = w_out_even[e].astype(BF16)
            sw = subln_w[e].reshape(1, A_VDIM)
            cw = conv_w[e]

            q, k, v, bg, u = _even_in(xp, nw[layer, 0], w_in, cos_p, sin_p, tm_p)
            o = _flash(lam, q, k, v, sw, nb, t, 1.0 - lam_init, flash_tile)
            xp_mid = _even_out(o, bg, u, None, cw, w_out, xp, nw[layer, 1], tm_p, t)

            qd, kd, vd, bgd, ud = _even_in(xs, nw[layer, 0], w_in, cos_d, sin_d, tm_d)
            od = _decode_attn(page_table, lam, qd.reshape(nd, 1, A_QK),
                              kd.reshape(nd, 1, A_QK), vd.reshape(nd, 1, A_V), sw,
                              ck, cv, e, 1.0 - lam_init, dec_pages)
            sc = state_conv[e]
            xs_mid = _even_out(od.reshape(nd, A_V), bgd, ud, (sc[:, 1], sc[:, 0]), cw,
                               w_out, xs, nw[layer, 1], tm_d, td)

            kp_l.append(k.reshape(nb, t, A_HEADS, 2, HEAD_DIM))
            vp_l.append(v.reshape(nb, t, A_HEADS, A_VDIM))
            cp_l.append(u.reshape(nb, t, B_WIDTH)[:, t - (CONV_W - 1):])
            ks_l.append(kd.reshape(nd, td, A_HEADS, 2, HEAD_DIM))
            vs_l.append(vd.reshape(nd, td, A_HEADS, A_VDIM))
            cs_l.append(jnp.concatenate([sc[:, 1:], ud[:, None, :]], axis=1))
        else:
            o_idx = layer // 2
            w_in = w_in_odd[o_idx].astype(BF16)
            w_out = w_out_odd[o_idx].astype(BF16)
            lb = lower_bounds[layer].reshape(1, HG_FDIM)
            gw = gnorm_w[o_idx].reshape(1, HG_DV)

            z = _odd_in(xp, nw[layer, 0], w_in, tm_p)
            og, sp = _gla(z, lb, gw, nb, t, gla_rows)
            xp_mid = _odd_out(og, w_out, xp, nw[layer, 1], tm_p)

            zd = _odd_in(xs, nw[layer, 0], w_in, tm_d)
            ogd, ss = _gla_decode(zd, lb, gw, state_hgrn[o_idx])
            xs_mid = _odd_out(ogd.reshape(nd, D_MODEL), w_out, xs, nw[layer, 1], tm_d)
            hp_l.append(sp)
            hs_l.append(ss)

        wi = w_ffn_in[layer].astype(BF16)
        wo = w_ffn_out[layer].astype(BF16)
        xp = _ffn(xp_mid, nw[layer, 2], wi, wo, nw[layer, 3], tm_p)
        xs = _ffn(xs_mid, nw[layer, 2], wi, wo, nw[layer, 3], tm_d)

    return (xp.reshape(nb, t, D_MODEL), xs.reshape(nd, td, D_MODEL),
            jnp.stack(kp_l), jnp.stack(vp_l), jnp.stack(cp_l), jnp.stack(hp_l),
            jnp.stack(ks_l), jnp.stack(vs_l), jnp.stack(cs_l), jnp.stack(hs_l))
```

```python
import functools
import math

import numpy as np
import jax
import jax.numpy as jnp
from jax import lax
from jax.experimental import pallas as pl
from jax.experimental.pallas import tpu as pltpu

D_MODEL = 1024
PAGE_SIZE = 128
HEAD_DIM = 64
A_HEADS = 4
A_QK = A_HEADS * 2 * HEAD_DIM
A_VDIM = 2 * HEAD_DIM
A_V = A_HEADS * A_VDIM
ATTN_SCALE = HEAD_DIM ** -0.5
ROPE_THETA = 10000.0
B_WIDTH = D_MODEL // 2
CONV_W = 3
EVEN_IN = 2 * A_QK + A_V + 3 * B_WIDTH
HG_HEADS = 8
HG_DK = 128
HG_FDIM = HG_HEADS * HG_DK
HG_DV = D_MODEL // HG_HEADS
ODD_IN = 2 * HG_FDIM + 2 * D_MODEL
D_FF = -(-8 * D_MODEL // (3 * 256)) * 256
EPS = 1e-6
LOG2E = math.log2(math.e)

LANES = 128
SUBLANES = 8
VMEM_LIMIT_BYTES = 56 * 1024 * 1024
GLA_CHUNK = 128
GLA_LEVELS = 7
GLA_SUM_LEVELS = 3
FFN_CHUNK = 256
FLASH_ROWS = 128

F32 = jnp.float32
BF16 = jnp.bfloat16


def _cparams(semantics):
    return pltpu.CompilerParams(dimension_semantics=semantics,
                                vmem_limit_bytes=VMEM_LIMIT_BYTES)


def _rms(x, w):
    return x * lax.rsqrt(jnp.mean(x * x, axis=-1, keepdims=True) + EPS) * w


def _sigmoid(x):
    return 1.0 / (1.0 + jnp.exp(-x))


def _tile_lanes(x, n):
    return x if n == 1 else jnp.concatenate([x] * n, axis=1)


def _layer_spec(stacked, li):
    shape = stacked.shape[1:]
    nd = len(shape)
    return pl.BlockSpec((None,) + shape, lambda *_: (li,) + (0,) * nd,
                        pipeline_mode=pl.Buffered(1))


def _const_spec(shape):
    nd = len(shape)
    return pl.BlockSpec(shape, lambda *_: (0,) * nd, pipeline_mode=pl.Buffered(1))


def _even_in_kernel(x_ref, nw_ref, w_ref, cos_ref, sin_ref,
                    q_ref, k_ref, v_ref, bg_ref, u_ref):
    xn = _rms(x_ref[...], nw_ref[...]).astype(BF16)
    cos = cos_ref[...]
    sin = sin_ref[...]
    lane = lax.broadcasted_iota(jnp.int32, cos.shape, 1)
    first_half = (lane % HEAD_DIM) < (HEAD_DIM // 2)

    def proj(c):
        return jnp.dot(xn, w_ref[:, c * A_QK:(c + 1) * A_QK],
                       preferred_element_type=F32)

    def rope(z, out_ref):
        for s in range(A_QK // LANES):
            zs = z[:, s * LANES:(s + 1) * LANES]
            swapped = jnp.where(first_half,
                                pltpu.roll(zs, LANES - HEAD_DIM // 2, 1),
                                pltpu.roll(zs, HEAD_DIM // 2, 1))
            out_ref[:, s * LANES:(s + 1) * LANES] = zs * cos + swapped * sin

    rope(proj(0), q_ref)
    rope(proj(1), k_ref)
    v_ref[...] = proj(2)
    bg_ref[...] = proj(3)
    u_ref[...] = proj(4) * proj(5)


def _even_in(x, nw, w, li, cos, sin, tm):
    n = x.shape[0]
    row = lambda i: (i, 0)
    pos_tiles = cos.shape[0] // tm
    pos = lambda i: (i % pos_tiles, 0)
    out = jax.ShapeDtypeStruct((n, A_QK), F32)
    return pl.pallas_call(
        _even_in_kernel,
        grid=(n // tm,),
        in_specs=[pl.BlockSpec((tm, D_MODEL), row),
                  _const_spec((1, D_MODEL)),
                  _layer_spec(w, li),
                  pl.BlockSpec((tm, LANES), pos),
                  pl.BlockSpec((tm, LANES), pos)],
        out_specs=[pl.BlockSpec((tm, A_QK), row)] * 5,
        out_shape=[out] * 5,
        compiler_params=_cparams(("parallel",)),
        name="even_in",
    )(x, nw, w, cos, sin)


def _flash_kernel(ii_ref, jj_ref, lam_ref, q_ref, k_ref, v_ref, sw_ref, o_ref,
                  qs_ref, m_ref, acc_ref, *, tq, lam_scale):
    pair = pl.program_id(2)
    i = ii_ref[pair]
    j = jj_ref[pair]

    @pl.when(j == 0)
    def _init():
        qs_ref[...] = (q_ref[...] * (ATTN_SCALE * LOG2E)).astype(BF16)
        m_ref[...] = jnp.full(m_ref.shape, -jnp.inf, F32)
        acc_ref[...] = jnp.zeros(acc_ref.shape, F32)

    def step(diagonal):
        kf = k_ref[...]
        lane = lax.broadcasted_iota(jnp.int32, kf.shape, 1)
        v = v_ref[...].astype(BF16)
        v_ext = jnp.concatenate([v, jnp.ones_like(v)], axis=1)
        tri = (lax.broadcasted_iota(jnp.int32, (FLASH_ROWS, FLASH_ROWS), 1)
               <= lax.broadcasted_iota(jnp.int32, (FLASH_ROWS, FLASH_ROWS), 0))
        for mp in range(2):
            keep = (lane < HEAD_DIM) if mp == 0 else (lane >= HEAD_DIM)
            k = jnp.where(keep, kf, 0.0).astype(BF16)
            for rc in range(tq // FLASH_ROWS):
                rows = slice(rc * FLASH_ROWS, (rc + 1) * FLASH_ROWS)
                first = diagonal and (rc + 1) * FLASH_ROWS <= tq // 2
                ncol = tq // 2 if first else tq
                s = lax.dot_general(qs_ref[rows, :], k[0:ncol], (((1,), (1,)), ((), ())),
                                    preferred_element_type=F32)
                if diagonal:
                    lo = rc * FLASH_ROWS
                    hi = lo + FLASH_ROWS
                    parts = [s[:, 0:lo]] if rc > 0 else []
                    parts.append(jnp.where(tri, s[:, lo:hi], -jnp.inf))
                    if hi < ncol:
                        parts.append(jnp.full((FLASH_ROWS, ncol - hi), -jnp.inf, F32))
                    s = jnp.concatenate(parts, axis=1)
                m_prev = m_ref[mp, rows, :]
                m_new = jnp.maximum(m_prev, jnp.max(s, axis=-1, keepdims=True))
                alpha = jnp.exp2(m_prev - m_new)
                p = jnp.exp2(s - _tile_lanes(m_new, ncol // LANES))
                pv = jnp.dot(p.astype(BF16), v_ext[0:ncol], preferred_element_type=F32)
                acc_ref[mp, rows, :] = _tile_lanes(alpha, 2) * acc_ref[mp, rows, :] + pv
                m_ref[mp, rows, :] = m_new

    @pl.when(j < i)
    def _off_diag():
        step(False)

    @pl.when(j == i)
    def _diag():
        step(True)
        a0 = acc_ref[0]
        a1 = acc_ref[1]
        o = (a0[:, 0:LANES] / a0[:, LANES:2 * LANES]
             - lam_ref[0] * (a1[:, 0:LANES] / a1[:, LANES:2 * LANES]))
        o_ref[...] = _rms(o, sw_ref[...]) * lam_scale


def _flash(lam, q, k, v, sw, nb, t, lam_scale, tq):
    nq = t // tq
    pairs = [(i, j) for i in range(nq) for j in range(i + 1)]
    ii = jnp.asarray(np.array([p[0] for p in pairs], np.int32))
    jj = jnp.asarray(np.array([p[1] for p in pairs], np.int32))
    kern = functools.partial(_flash_kernel, tq=tq, lam_scale=lam_scale)
    qmap = lambda b, h, p, ii, jj: (b * nq + ii[p], h)
    kmap = lambda b, h, p, ii, jj: (b * nq + jj[p], h)
    grid_spec = pltpu.PrefetchScalarGridSpec(
        num_scalar_prefetch=2,
        grid=(nb, A_HEADS, len(pairs)),
        in_specs=[pl.BlockSpec(memory_space=pltpu.SMEM),
                  pl.BlockSpec((tq, LANES), qmap),
                  pl.BlockSpec((tq, LANES), kmap),
                  pl.BlockSpec((tq, LANES), kmap),
                  pl.BlockSpec((1, LANES), lambda b, h, p, ii, jj: (0, 0))],
        out_specs=pl.BlockSpec((tq, LANES), qmap),
        scratch_shapes=[pltpu.VMEM((tq, LANES), BF16),
                        pltpu.VMEM((2, tq, LANES), F32),
                        pltpu.VMEM((2, tq, 2 * LANES), F32)])
    return pl.pallas_call(
        kern,
        grid_spec=grid_spec,
        out_shape=jax.ShapeDtypeStruct((nb * t, A_V), F32),
        compiler_params=_cparams(("parallel", "parallel", "arbitrary")),
        name="flash_diff_attn",
    )(ii, jj, lam, q, k, v, sw)


DEC_ROWS = 16


def _decode_attn_kernel(pt_ref, lam_ref, q_ref, kc_ref, vc_ref, sw_ref, *refs,
                        pages, lam_scale):
    del pt_ref
    kp = refs[0:pages]
    vp = refs[pages:2 * pages]
    o_ref = refs[2 * pages]
    qbd_ref, m_ref, l_ref, acc_ref = refs[2 * pages + 1:]
    j = pl.program_id(1)

    @pl.when(j == 0)
    def _init():
        q = jnp.broadcast_to(q_ref[0] * (ATTN_SCALE * LOG2E), (DEC_ROWS, A_QK))
        row = lax.broadcasted_iota(jnp.int32, q.shape, 0)
        blk = lax.broadcasted_iota(jnp.int32, q.shape, 1) // HEAD_DIM
        qbd_ref[...] = jnp.where(row == blk, q, 0.0)
        m_ref[...] = jnp.full(m_ref.shape, -jnp.inf, F32)
        l_ref[...] = jnp.zeros(l_ref.shape, F32)
        acc_ref[...] = jnp.zeros(acc_ref.shape, F32)

    qbd = qbd_ref[...].astype(BF16)
    s = jnp.concatenate(
        [jnp.dot(qbd, kp[p][...].astype(BF16), preferred_element_type=F32)
         for p in range(pages)], axis=1)
    m_prev = m_ref[...]
    m_new = jnp.maximum(m_prev, jnp.max(s, axis=-1, keepdims=True))
    alpha = jnp.exp2(m_prev - m_new)
    p_all = jnp.exp2(s - _tile_lanes(m_new, pages))
    l_ref[...] = alpha * l_ref[...] + jnp.sum(p_all, axis=-1, keepdims=True)
    p_bf = p_all.astype(BF16)
    for h in range(A_HEADS):
        pv = jnp.zeros((DEC_ROWS, A_VDIM), F32)
        for p in range(pages):
            v_h = vp[p][pl.ds(h, PAGE_SIZE, stride=A_HEADS), :]
            pv = pv + jnp.dot(p_bf[:, p * PAGE_SIZE:(p + 1) * PAGE_SIZE],
                              v_h.astype(BF16), preferred_element_type=F32)
        acc_ref[h] = alpha * acc_ref[h] + pv
    m_ref[...] = m_new

    @pl.when(j == pl.num_programs(1) - 1)
    def _finish():
        s_cur = jnp.sum(qbd_ref[...] * kc_ref[0], axis=-1, keepdims=True)
        m_old = m_ref[...]
        m_fin = jnp.maximum(m_old, s_cur)
        a_fin = jnp.exp2(m_old - m_fin)
        p_cur = jnp.exp2(s_cur - m_fin)
        l_fin = a_fin * l_ref[...] + p_cur
        lam = lam_ref[0]
        sw = sw_ref[...]
        vc = vc_ref[0]
        for h in range(A_HEADS):
            cols = slice(h * A_VDIM, (h + 1) * A_VDIM)
            a = (a_fin * acc_ref[h] + p_cur * vc[:, cols]) / l_fin
            o = a[2 * h:2 * h + 1, :] - lam * a[2 * h + 1:2 * h + 2, :]
            o_ref[0, :, cols] = _rms(o, sw) * lam_scale


def _decode_attn(page_table, lam, qd, kd, vd, sw, cache_kt, cache_vf, e, lam_scale,
                 pages):
    nb, n_pages = page_table.shape
    kern = functools.partial(_decode_attn_kernel, pages=pages, lam_scale=lam_scale)
    tok = lambda b, j, pt: (b, 0, 0)

    def page_spec(p):
        return pl.BlockSpec((None, None, A_QK, PAGE_SIZE),
                            lambda b, j, pt: (e, pt[b, j * pages + p], 0, 0))

    grid_spec = pltpu.PrefetchScalarGridSpec(
        num_scalar_prefetch=1,
        grid=(nb, n_pages // pages),
        in_specs=([pl.BlockSpec(memory_space=pltpu.SMEM),
                   pl.BlockSpec((1, 1, A_QK), tok),
                   pl.BlockSpec((1, 1, A_QK), tok),
                   pl.BlockSpec((1, 1, A_V), tok),
                   pl.BlockSpec((1, LANES), lambda b, j, pt: (0, 0))]
                  + [page_spec(p) for p in range(pages)] * 2),
        out_specs=pl.BlockSpec((1, 1, A_V), tok),
        scratch_shapes=[pltpu.VMEM((DEC_ROWS, A_QK), F32),
                        pltpu.VMEM((DEC_ROWS, LANES), F32),
                        pltpu.VMEM((DEC_ROWS, LANES), F32),
                        pltpu.VMEM((A_HEADS, DEC_ROWS, A_VDIM), F32)])
    return pl.pallas_call(
        kern,
        grid_spec=grid_spec,
        out_shape=jax.ShapeDtypeStruct((nb, 1, A_V), F32),
        compiler_params=_cparams(("parallel", "arbitrary")),
        name="decode_diff_attn",
    )(page_table, lam, qd, kd, vd, sw, *([cache_kt] * pages), *([cache_vf] * pages))


def _even_out_kernel(*refs, tm, seq_tiles, explicit_prev):
    if explicit_prev:
        (o_ref, bg_ref, u_ref, um1_ref, um2_ref, cw_ref, w_ref, x_ref, nw_ref,
         out_ref) = refs
        u = u_ref[...]
        um1 = um1_ref[...]
        um2 = um2_ref[...]
    else:
        o_ref, bg_ref, u_ref, halo_ref, cw_ref, w_ref, x_ref, nw_ref, out_ref = refs
        u = u_ref[...]
        halo = halo_ref[...]
        halo = jnp.where(pl.program_id(0) % seq_tiles == 0, 0.0, halo)
        row = lax.broadcasted_iota(jnp.int32, u.shape, 0)
        h1 = halo[SUBLANES - 1:SUBLANES, :]
        h2 = halo[SUBLANES - 2:SUBLANES - 1, :]
        um1 = jnp.where(row == 0, h1, pltpu.roll(u, 1, 0))
        um2 = jnp.where(row == 0, h2, jnp.where(row == 1, h1, pltpu.roll(u, 2, 0)))
    cw = cw_ref[...]
    conv = um2 * cw[0:1, :] + um1 * cw[1:2, :] + u * cw[2:3, :]
    yb = bg_ref[...] * conv
    y = (jnp.dot(o_ref[...].astype(BF16), w_ref[0:A_V, :], preferred_element_type=F32)
         + jnp.dot(yb.astype(BF16), w_ref[A_V:A_V + B_WIDTH, :],
                   preferred_element_type=F32))
    out_ref[...] = x_ref[...] + _rms(y, nw_ref[...])


def _even_out(o, bg, u, prev, cw, w, li, x, nw, tm, seq_len):
    n = x.shape[0]
    row = lambda i: (i, 0)
    half = pl.BlockSpec((tm, B_WIDTH), row)
    explicit_prev = isinstance(prev, tuple)
    if explicit_prev:
        prev_args = list(prev)
        prev_specs = [half, half]
    else:
        prev_args = [u]
        prev_specs = [pl.BlockSpec(
            (SUBLANES, B_WIDTH),
            lambda i: (jnp.maximum(i * (tm // SUBLANES) - 1, 0), 0))]
    kern = functools.partial(_even_out_kernel, tm=tm, seq_tiles=seq_len // tm,
                             explicit_prev=explicit_prev)
    return pl.pallas_call(
        kern,
        grid=(n // tm,),
        in_specs=([half, half, half] + prev_specs
                  + [_const_spec((CONV_W, B_WIDTH)),
                     _layer_spec(w, li),
                     pl.BlockSpec((tm, D_MODEL), row),
                     _const_spec((1, D_MODEL))]),
        out_specs=pl.BlockSpec((tm, D_MODEL), row),
        out_shape=jax.ShapeDtypeStruct((n, D_MODEL), F32),
        compiler_params=_cparams(("parallel",)),
        name="even_out",
    )(o, bg, u, *prev_args, cw, w, x, nw)


def _ffn_kernel(x_ref, nw_in_ref, wi_ref, wo_ref, nw_out_ref, out_ref):
    x = x_ref[...]
    xn = _rms(x, nw_in_ref[...]).astype(BF16)
    y = jnp.zeros(x.shape, F32)
    for c in range(D_FF // FFN_CHUNK):
        lo = c * FFN_CHUNK
        gate = jnp.dot(xn, wi_ref[:, lo:lo + FFN_CHUNK], preferred_element_type=F32)
        up = jnp.dot(xn, wi_ref[:, D_FF + lo:D_FF + lo + FFN_CHUNK],
                     preferred_element_type=F32)
        act = (gate * _sigmoid(gate) * up).astype(BF16)
        y = y + jnp.dot(act, wo_ref[lo:lo + FFN_CHUNK, :], preferred_element_type=F32)
    out_ref[...] = x + _rms(y, nw_out_ref[...])


def _ffn(x, nw_in, wi, wo, li, nw_out, tm):
    n = x.shape[0]
    row = lambda i: (i, 0)
    return pl.pallas_call(
        _ffn_kernel,
        grid=(n // tm,),
        in_specs=[pl.BlockSpec((tm, D_MODEL), row),
                  _const_spec((1, D_MODEL)),
                  _layer_spec(wi, li),
                  _layer_spec(wo, li),
                  _const_spec((1, D_MODEL))],
        out_specs=pl.BlockSpec((tm, D_MODEL), row),
        out_shape=jax.ShapeDtypeStruct((n, D_MODEL), F32),
        compiler_params=_cparams(("parallel",)),
        name="ffn",
    )(x, nw_in, wi, wo, nw_out)


def _odd_in_kernel(x_ref, nw_ref, w_ref, lb_ref, z_ref):
    xn = _rms(x_ref[...], nw_ref[...]).astype(BF16)

    def proj(c):
        return jnp.dot(xn, w_ref[:, c * D_MODEL:(c + 1) * D_MODEL],
                       preferred_element_type=F32)

    zq = proj(0)
    z_ref[:, 0:HG_FDIM] = zq * _sigmoid(zq)
    lb = lb_ref[...]
    z_ref[:, HG_FDIM:2 * HG_FDIM] = lb + (1.0 - lb) * _sigmoid(proj(1))
    z_ref[:, 2 * HG_FDIM:2 * HG_FDIM + D_MODEL] = proj(2)
    zg = proj(3)
    z_ref[:, 2 * HG_FDIM + D_MODEL:ODD_IN] = zg * _sigmoid(zg)


def _odd_in(x, nw, w, li, lb, tm):
    n = x.shape[0]
    row = lambda i: (i, 0)
    return pl.pallas_call(
        _odd_in_kernel,
        grid=(n // tm,),
        in_specs=[pl.BlockSpec((tm, D_MODEL), row),
                  _const_spec((1, D_MODEL)),
                  _layer_spec(w, li),
                  _const_spec((1, HG_FDIM))],
        out_specs=pl.BlockSpec((tm, ODD_IN), row),
        out_shape=jax.ShapeDtypeStruct((n, ODD_IN), F32),
        compiler_params=_cparams(("parallel",)),
        name="odd_in",
    )(x, nw, w, lb)


def _gla_tables():
    c = GLA_CHUNK
    t = np.arange(c)[:, None]
    r = np.arange(c)[None, :]
    mats = []
    for lev in range(GLA_SUM_LEVELS):
        half = 1 << lev
        mid = (t // (2 * half)) * (2 * half) + half
        mats.append(np.where(t >= mid, (r >= mid) & (r <= t), (r > t) & (r < mid)))
    mats.append(r <= t)
    sums = np.concatenate(mats, axis=0).astype(np.float32)
    high_bit = np.floor(np.log2(np.maximum(t ^ r, 1))).astype(np.int32)
    level = np.where(t > r, high_bit, np.where(t == r, GLA_LEVELS, GLA_LEVELS + 1))
    sign = np.stack([np.where((t >> lev) & 1 == 1, 1.0, -1.0) * np.ones((1, HG_DK))
                     for lev in range(GLA_SUM_LEVELS, GLA_LEVELS)]).astype(np.float32)
    return (jnp.asarray(sums, BF16), jnp.asarray(level.astype(np.int32)),
            jnp.asarray(sign))


def _gla_kernel(q_ref, f_ref, v_ref, g_ref, gw_ref, sums_ref, level_ref, sign_ref,
                og_ref, sout_ref, st_ref, o_s, *, rows):
    c = GLA_CHUNK
    tb = pl.program_id(2)

    @pl.when(tb == 0)
    def _zero_state():
        st_ref[...] = jnp.zeros(st_ref.shape, F32)

    q_all = q_ref[...]
    fg = f_ref[...]
    logf_all = jnp.log(fg) * LOG2E
    k_all = 1.0 - fg
    v_all = v_ref[...]
    level = level_ref[...]
    row_id = lax.broadcasted_iota(jnp.int32, (c, HG_DK), 0)
    contract_last = (((1,), (1,)), ((), ()))

    st = st_ref[...]
    for ch in range(rows // c):
        sl = slice(ch * c, (ch + 1) * c)
        q = q_all[sl]
        k = k_all[sl]
        v = v_all[sl].astype(BF16)
        logf = logf_all[sl]
        hi = logf.astype(BF16)
        lo = (logf - hi.astype(F32)).astype(BF16)
        w = jnp.dot(sums_ref[...], jnp.concatenate([hi, lo], axis=1),
                    preferred_element_type=F32)
        w = w[:, 0:HG_DK] + w[:, HG_DK:2 * HG_DK]
        cum = w[GLA_SUM_LEVELS * c:(GLA_SUM_LEVELS + 1) * c]
        a = jnp.zeros((c, c), F32)
        for lev in range(GLA_LEVELS):
            if lev < GLA_SUM_LEVELS:
                expo = w[lev * c:(lev + 1) * c]
            else:
                half = 1 << lev
                at_split = jnp.concatenate(
                    [jnp.broadcast_to(cum[b0 + half - 1:b0 + half, :], (2 * half, HG_DK))
                     for b0 in range(0, c, 2 * half)], axis=0)
                expo = (cum - at_split) * sign_ref[lev - GLA_SUM_LEVELS]
            upper = ((row_id >> lev) & 1) == 1
            y = (jnp.where(upper, q, k) * jnp.exp2(expo)).astype(BF16)
            a = jnp.where(level == lev,
                          lax.dot_general(y, y, contract_last,
                                          preferred_element_type=F32), a)
        a = jnp.where(level == GLA_LEVELS,
                      lax.dot_general(q.astype(BF16), k.astype(BF16), contract_last,
                                      preferred_element_type=F32), a)
        o_intra = jnp.dot(a.astype(BF16), v, preferred_element_type=F32)
        total = cum[c - 1:c, :]
        qh = (q * jnp.exp2(cum)).astype(BF16)
        kh = (k * jnp.exp2(total - cum)).astype(BF16)
        o_inter = lax.dot_general(qh, st.astype(BF16), contract_last,
                                  preferred_element_type=F32)
        st = st * jnp.exp2(total) + lax.dot_general(v, kh, (((0,), (0,)), ((), ())),
                                                    preferred_element_type=F32)
        o_s[sl, :] = o_inter + o_intra
    st_ref[...] = st
    og_ref[...] = _rms(o_s[...], gw_ref[...]) * g_ref[...]

    @pl.when(tb == pl.num_programs(2) - 1)
    def _store_state():
        sout_ref[0, 0] = st.T


def _gla(z, gw, nseq, t, rows):
    nt = t // rows
    sums, level, sign = _gla_tables()

    def col_spec(base):
        return pl.BlockSpec((rows, HG_DK),
                            lambda s, h, tb: (s * nt + tb, base + h))

    kern = functools.partial(_gla_kernel, rows=rows)
    return pl.pallas_call(
        kern,
        grid=(nseq, HG_HEADS, nt),
        in_specs=[col_spec(0), col_spec(HG_HEADS), col_spec(2 * HG_HEADS),
                  col_spec(3 * HG_HEADS),
                  pl.BlockSpec((1, HG_DV), lambda s, h, tb: (0, 0)),
                  _const_spec(sums.shape), _const_spec(level.shape),
                  _const_spec(sign.shape)],
        out_specs=[pl.BlockSpec((rows, HG_DV), lambda s, h, tb: (s * nt + tb, h)),
                   pl.BlockSpec((1, 1, HG_DK, HG_DV), lambda s, h, tb: (s, h, 0, 0))],
        out_shape=[jax.ShapeDtypeStruct((nseq * t, D_MODEL), F32),
                   jax.ShapeDtypeStruct((nseq, HG_HEADS, HG_DK, HG_DV), F32)],
        scratch_shapes=[pltpu.VMEM((HG_DV, HG_DK), F32),
                        pltpu.VMEM((rows, HG_DV), F32)],
        compiler_params=_cparams(("parallel", "parallel", "arbitrary")),
        name="hgrn2_gla",
    )(z, z, z, z, gw, sums, level, sign)


def _gla_decode_kernel(z_ref, gw_ref, s0_ref, og_ref, sout_ref):
    z = z_ref[0]
    eye = (lax.broadcasted_iota(jnp.int32, (HG_DK, HG_DK), 0)
           == lax.broadcasted_iota(jnp.int32, (HG_DK, HG_DK), 1))

    def column(row_vec):
        return jnp.sum(jnp.where(eye, row_vec, 0.0), axis=-1, keepdims=True)

    gw = gw_ref[...]
    for h in range(HG_HEADS):
        cols = slice(h * HG_DK, (h + 1) * HG_DK)
        q = z[:, cols]
        fg = z[:, HG_FDIM + h * HG_DK:HG_FDIM + (h + 1) * HG_DK]
        v = z[:, 2 * HG_FDIM + h * HG_DV:2 * HG_FDIM + (h + 1) * HG_DV]
        gate = z[:, 2 * HG_FDIM + D_MODEL + h * HG_DV:2 * HG_FDIM + D_MODEL + (h + 1) * HG_DV]
        s_new = column(fg) * s0_ref[h] + column(1.0 - fg) * v
        sout_ref[0, h] = s_new
        o = jnp.sum(column(q) * s_new, axis=0, keepdims=True)
        og_ref[0, :, cols] = _rms(o, gw) * gate


def _gla_decode(z, gw, states, li):
    nseq = z.shape[0]
    tok = lambda s: (s, 0, 0)
    heads = (HG_HEADS, HG_DK, HG_DV)
    return pl.pallas_call(
        _gla_decode_kernel,
        grid=(nseq,),
        in_specs=[pl.BlockSpec((1, 1, ODD_IN), tok),
                  _const_spec((1, HG_DV)),
                  pl.BlockSpec((None, None) + heads, lambda s: (li, s, 0, 0, 0))],
        out_specs=[pl.BlockSpec((1, 1, D_MODEL), tok),
                   pl.BlockSpec((1,) + heads, lambda s: (s, 0, 0, 0))],
        out_shape=[jax.ShapeDtypeStruct((nseq, 1, D_MODEL), F32),
                   jax.ShapeDtypeStruct((nseq,) + heads, F32)],
        compiler_params=_cparams(("parallel",)),
        name="hgrn2_decode",
    )(z.reshape(nseq, 1, ODD_IN), gw, states)


def _odd_out_kernel(a_ref, w_ref, x_ref, nw_ref, out_ref):
    y = jnp.dot(a_ref[...].astype(BF16), w_ref[...], preferred_element_type=F32)
    out_ref[...] = x_ref[...] + _rms(y, nw_ref[...])


def _odd_out(a, w, li, x, nw, tm):
    n = x.shape[0]
    row = lambda i: (i, 0)
    return pl.pallas_call(
        _odd_out_kernel,
        grid=(n // tm,),
        in_specs=[pl.BlockSpec((tm, D_MODEL), row),
                  _layer_spec(w, li),
                  pl.BlockSpec((tm, D_MODEL), row),
                  _const_spec((1, D_MODEL))],
        out_specs=pl.BlockSpec((tm, D_MODEL), row),
        out_shape=jax.ShapeDtypeStruct((n, D_MODEL), F32),
        compiler_params=_cparams(("parallel",)),
        name="odd_out",
    )(a, w, x, nw)


def _rope_tables(pos):
    half = HEAD_DIM // 2
    inv = ROPE_THETA ** (-jnp.arange(half, dtype=F32) / half)
    ang = pos.astype(F32)[:, None] * inv[None, :]
    cos = jnp.tile(jnp.cos(ang), (1, LANES // half))
    sin = jnp.tile(jnp.concatenate([-jnp.sin(ang), jnp.sin(ang)], axis=1),
                   (1, LANES // HEAD_DIM))
    return cos, sin


def kernel(x_prompt, x_sample, cache_k, cache_v, state_conv, state_hgrn, page_table,
           norm_w, w_in_even, w_out_even, conv_w, lambda_q1, lambda_k1, lambda_q2,
           lambda_k2, subln_w, w_in_odd, w_out_odd, hgrn_lb, gnorm_w, w_ffn_in,
           w_ffn_out):
    nb, t, _ = x_prompt.shape
    nd, td, _ = x_sample.shape
    assert td == 1
    depth = norm_w.shape[0]
    n_pages = page_table.shape[1]
    past_len = n_pages * PAGE_SIZE
    tm_p = 512
    tm_d = nd
    dec_pages = 16
    gla_rows = 2048
    flash_tile = 1024

    cos_p, sin_p = _rope_tables(jnp.arange(t))
    cos_d, sin_d = _rope_tables(jnp.full((nd,), past_len))
    p_lb = jax.nn.softmax(hgrn_lb.astype(F32), axis=0)
    lower_bounds = jnp.cumsum(p_lb, axis=0) - p_lb[0]

    n_even, n_phys = cache_k.shape[:2]
    ck = jnp.transpose(cache_k, (0, 1, 3, 4, 5, 2)).reshape(n_even, n_phys, A_QK, PAGE_SIZE)
    cv = cache_v.reshape(n_even, n_phys, PAGE_SIZE * A_HEADS, A_VDIM)
    nw = norm_w.reshape(depth, 4, 1, D_MODEL)
    w_in_e, w_out_e = w_in_even.astype(BF16), w_out_even.astype(BF16)
    w_in_o, w_out_o = w_in_odd.astype(BF16), w_out_odd.astype(BF16)
    w_ffn_i, w_ffn_o = w_ffn_in.astype(BF16), w_ffn_out.astype(BF16)

    xp = x_prompt.reshape(nb * t, D_MODEL)
    xs = x_sample.reshape(nd, D_MODEL)
    kp_l, vp_l, cp_l, hp_l = [], [], [], []
    ks_l, vs_l, cs_l, hs_l = [], [], [], []
    for layer in range(depth):
        if layer % 2 == 0:
            e = layer // 2
            lam_init = 0.8 - 0.6 * math.exp(-0.3 * layer)
            lam = (jnp.exp(jnp.sum(lambda_q1[e].astype(F32) * lambda_k1[e].astype(F32)))
                   - jnp.exp(jnp.sum(lambda_q2[e].astype(F32) * lambda_k2[e].astype(F32)))
                   + lam_init).reshape(1)
            sw = subln_w[e].reshape(1, A_VDIM)
            cw = conv_w[e]

            q, k, v, bg, u = _even_in(xp, nw[layer, 0], w_in_e, e, cos_p, sin_p, tm_p)
            o = _flash(lam, q, k, v, sw, nb, t, 1.0 - lam_init, flash_tile)
            xp_mid = _even_out(o, bg, u, None, cw, w_out_e, e, xp, nw[layer, 1], tm_p, t)

            qd, kd, vd, bgd, ud = _even_in(xs, nw[layer, 0], w_in_e, e, cos_d, sin_d, tm_d)
            od = _decode_attn(page_table, lam, qd.reshape(nd, 1, A_QK),
                              kd.reshape(nd, 1, A_QK), vd.reshape(nd, 1, A_V), sw,
                              ck, cv, e, 1.0 - lam_init, dec_pages)
            sc = state_conv[e]
            xs_mid = _even_out(od.reshape(nd, A_V), bgd, ud, (sc[:, 1], sc[:, 0]), cw,
                               w_out_e, e, xs, nw[layer, 1], tm_d, td)

            kp_l.append(k.reshape(nb, t, A_HEADS, 2, HEAD_DIM))
            vp_l.append(v.reshape(nb, t, A_HEADS, A_VDIM))
            cp_l.append(u.reshape(nb, t, B_WIDTH)[:, t - (CONV_W - 1):])
            ks_l.append(kd.reshape(nd, td, A_HEADS, 2, HEAD_DIM))
            vs_l.append(vd.reshape(nd, td, A_HEADS, A_VDIM))
            cs_l.append(jnp.concatenate([sc[:, 1:], ud[:, None, :]], axis=1))
        else:
            o_idx = layer // 2
            lb = lower_bounds[layer].reshape(1, HG_FDIM)
            gw = gnorm_w[o_idx].reshape(1, HG_DV)

            z = _odd_in(xp, nw[layer, 0], w_in_o, o_idx, lb, tm_p)
            og, sp = _gla(z, gw, nb, t, gla_rows)
            xp_mid = _odd_out(og, w_out_o, o_idx, xp, nw[layer, 1], tm_p)

            zd = _odd_in(xs, nw[layer, 0], w_in_o, o_idx, lb, tm_d)
            ogd, ss = _gla_decode(zd, gw, state_hgrn, o_idx)
            xs_mid = _odd_out(ogd.reshape(nd, D_MODEL), w_out_o, o_idx, xs, nw[layer, 1], tm_d)
            hp_l.append(sp)
            hs_l.append(ss)

        xp = _ffn(xp_mid, nw[layer, 2], w_ffn_i, w_ffn_o, layer, nw[layer, 3], tm_p)
        xs = _ffn(xs_mid, nw[layer, 2], w_ffn_i, w_ffn_o, layer, nw[layer, 3], tm_d)

    return (xp.reshape(nb, t, D_MODEL), xs.reshape(nd, td, D_MODEL),
            jnp.stack(kp_l), jnp.stack(vp_l), jnp.stack(cp_l), jnp.stack(hp_l),
            jnp.stack(ks_l), jnp.stack(vs_l), jnp.stack(cs_l), jnp.stack(hs_l))
```

```python
import functools
import math

import numpy as np
import jax
import jax.numpy as jnp
from jax import lax
from jax.experimental import pallas as pl
from jax.experimental.pallas import tpu as pltpu

D_MODEL = 1024
PAGE_SIZE = 128
HEAD_DIM = 64
A_HEADS = 4
A_QK = A_HEADS * 2 * HEAD_DIM
A_VDIM = 2 * HEAD_DIM
A_V = A_HEADS * A_VDIM
ATTN_SCALE = HEAD_DIM ** -0.5
ROPE_THETA = 10000.0
B_WIDTH = D_MODEL // 2
CONV_W = 3
EVEN_IN = 2 * A_QK + A_V + 3 * B_WIDTH
HG_HEADS = 8
HG_DK = 128
HG_FDIM = HG_HEADS * HG_DK
HG_DV = D_MODEL // HG_HEADS
ODD_IN = 2 * HG_FDIM + 2 * D_MODEL
D_FF = -(-8 * D_MODEL // (3 * 256)) * 256
EPS = 1e-6
LOG2E = math.log2(math.e)

LANES = 128
SUBLANES = 8
VMEM_LIMIT_BYTES = 56 * 1024 * 1024
GLA_CHUNK = 128
GLA_LEVELS = 7
GLA_SUM_LEVELS = 3
FFN_CHUNK = 256
FLASH_ROWS = 128

F32 = jnp.float32
BF16 = jnp.bfloat16


def _cparams(semantics):
    return pltpu.CompilerParams(dimension_semantics=semantics,
                                vmem_limit_bytes=VMEM_LIMIT_BYTES)


def _rms(x, w):
    return x * lax.rsqrt(jnp.mean(x * x, axis=-1, keepdims=True) + EPS) * w


def _sigmoid(x):
    return 1.0 / (1.0 + jnp.exp(-x))


def _tile_lanes(x, n):
    return x if n == 1 else jnp.concatenate([x] * n, axis=1)


def _layer_spec(stacked, li):
    shape = stacked.shape[1:]
    nd = len(shape)
    return pl.BlockSpec((None,) + shape, lambda *_: (li,) + (0,) * nd,
                        pipeline_mode=pl.Buffered(1))


def _const_spec(shape):
    nd = len(shape)
    return pl.BlockSpec(shape, lambda *_: (0,) * nd, pipeline_mode=pl.Buffered(1))


def _even_in_kernel(x_ref, nw_ref, w_ref, cos_ref, sin_ref,
                    q_ref, k_ref, v_ref, bg_ref, u_ref):
    xn = _rms(x_ref[...], nw_ref[...]).astype(BF16)
    cos = cos_ref[...]
    sin = sin_ref[...]
    lane = lax.broadcasted_iota(jnp.int32, cos.shape, 1)
    first_half = (lane % HEAD_DIM) < (HEAD_DIM // 2)

    def proj(c):
        return jnp.dot(xn, w_ref[:, c * A_QK:(c + 1) * A_QK],
                       preferred_element_type=F32)

    def rope(z, out_ref):
        for s in range(A_QK // LANES):
            zs = z[:, s * LANES:(s + 1) * LANES]
            swapped = jnp.where(first_half,
                                pltpu.roll(zs, LANES - HEAD_DIM // 2, 1),
                                pltpu.roll(zs, HEAD_DIM // 2, 1))
            out_ref[:, s * LANES:(s + 1) * LANES] = zs * cos + swapped * sin

    rope(proj(0), q_ref)
    rope(proj(1), k_ref)
    v_ref[...] = proj(2)
    bg_ref[...] = proj(3)
    u_ref[...] = proj(4) * proj(5)


def _even_in(x, nw, w, li, cos, sin, tm):
    n = x.shape[0]
    row = lambda i: (i, 0)
    pos_tiles = cos.shape[0] // tm
    pos = lambda i: (i % pos_tiles, 0)
    out = jax.ShapeDtypeStruct((n, A_QK), F32)
    return pl.pallas_call(
        _even_in_kernel,
        grid=(n // tm,),
        in_specs=[pl.BlockSpec((tm, D_MODEL), row),
                  _const_spec((1, D_MODEL)),
                  _layer_spec(w, li),
                  pl.BlockSpec((tm, LANES), pos),
                  pl.BlockSpec((tm, LANES), pos)],
        out_specs=[pl.BlockSpec((tm, A_QK), row)] * 5,
        out_shape=[out] * 5,
        compiler_params=_cparams(("parallel",)),
        name="even_in",
    )(x, nw, w, cos, sin)


def _flash_kernel(ii_ref, jj_ref, lam_ref, q_ref, k_ref, v_ref, sw_ref, o_ref,
                  qs_ref, m_ref, acc_ref, *, tq, lam_scale):
    pair = pl.program_id(2)
    i = ii_ref[pair]
    j = jj_ref[pair]

    @pl.when(j == 0)
    def _init():
        qs_ref[...] = (q_ref[...] * (ATTN_SCALE * LOG2E)).astype(BF16)
        m_ref[...] = jnp.full(m_ref.shape, -jnp.inf, F32)
        acc_ref[...] = jnp.zeros(acc_ref.shape, F32)

    def step(diagonal):
        kf = k_ref[...]
        lane = lax.broadcasted_iota(jnp.int32, kf.shape, 1)
        v = v_ref[...].astype(BF16)
        v_ext = jnp.concatenate([v, jnp.ones_like(v)], axis=1)
        tri = (lax.broadcasted_iota(jnp.int32, (FLASH_ROWS, FLASH_ROWS), 1)
               <= lax.broadcasted_iota(jnp.int32, (FLASH_ROWS, FLASH_ROWS), 0))
        for mp in range(2):
            keep = (lane < HEAD_DIM) if mp == 0 else (lane >= HEAD_DIM)
            k = jnp.where(keep, kf, 0.0).astype(BF16)
            for rc in range(tq // FLASH_ROWS):
                rows = slice(rc * FLASH_ROWS, (rc + 1) * FLASH_ROWS)
                first = diagonal and (rc + 1) * FLASH_ROWS <= tq // 2
                ncol = tq // 2 if first else tq
                s = lax.dot_general(qs_ref[rows, :], k[0:ncol], (((1,), (1,)), ((), ())),
                                    preferred_element_type=F32)
                if diagonal:
                    lo = rc * FLASH_ROWS
                    hi = lo + FLASH_ROWS
                    parts = [s[:, 0:lo]] if rc > 0 else []
                    parts.append(jnp.where(tri, s[:, lo:hi], -jnp.inf))
                    if hi < ncol:
                        parts.append(jnp.full((FLASH_ROWS, ncol - hi), -jnp.inf, F32))
                    s = jnp.concatenate(parts, axis=1)
                m_prev = m_ref[mp, rows, :]
                m_new = jnp.maximum(m_prev, jnp.max(s, axis=-1, keepdims=True))
                alpha = jnp.exp2(m_prev - m_new)
                p = jnp.exp2(s - _tile_lanes(m_new, ncol // LANES))
                pv = jnp.dot(p.astype(BF16), v_ext[0:ncol], preferred_element_type=F32)
                acc_ref[mp, rows, :] = _tile_lanes(alpha, 2) * acc_ref[mp, rows, :] + pv
                m_ref[mp, rows, :] = m_new

    @pl.when(j < i)
    def _off_diag():
        step(False)

    @pl.when(j == i)
    def _diag():
        step(True)
        a0 = acc_ref[0]
        a1 = acc_ref[1]
        o = (a0[:, 0:LANES] / a0[:, LANES:2 * LANES]
             - lam_ref[0] * (a1[:, 0:LANES] / a1[:, LANES:2 * LANES]))
        o_ref[...] = _rms(o, sw_ref[...]) * lam_scale


def _flash(lam, q, k, v, sw, nb, t, lam_scale, tq):
    nq = t // tq
    pairs = [(i, j) for i in range(nq) for j in range(i + 1)]
    ii = jnp.asarray(np.array([p[0] for p in pairs], np.int32))
    jj = jnp.asarray(np.array([p[1] for p in pairs], np.int32))
    kern = functools.partial(_flash_kernel, tq=tq, lam_scale=lam_scale)
    qmap = lambda b, h, p, ii, jj: (b * nq + ii[p], h)
    kmap = lambda b, h, p, ii, jj: (b * nq + jj[p], h)
    grid_spec = pltpu.PrefetchScalarGridSpec(
        num_scalar_prefetch=2,
        grid=(nb, A_HEADS, len(pairs)),
        in_specs=[pl.BlockSpec(memory_space=pltpu.SMEM),
                  pl.BlockSpec((tq, LANES), qmap),
                  pl.BlockSpec((tq, LANES), kmap),
                  pl.BlockSpec((tq, LANES), kmap),
                  pl.BlockSpec((1, LANES), lambda b, h, p, ii, jj: (0, 0))],
        out_specs=pl.BlockSpec((tq, LANES), qmap),
        scratch_shapes=[pltpu.VMEM((tq, LANES), BF16),
                        pltpu.VMEM((2, tq, LANES), F32),
                        pltpu.VMEM((2, tq, 2 * LANES), F32)])
    return pl.pallas_call(
        kern,
        grid_spec=grid_spec,
        out_shape=jax.ShapeDtypeStruct((nb * t, A_V), F32),
        compiler_params=_cparams(("parallel", "parallel", "arbitrary")),
        name="flash_diff_attn",
    )(ii, jj, lam, q, k, v, sw)


DEC_ROWS = 16


def _decode_attn_kernel(pt_ref, lam_ref, q_ref, kc_ref, vc_ref, sw_ref, *refs,
                        pages, lam_scale):
    del pt_ref
    kp = refs[0:pages]
    vp = refs[pages:2 * pages]
    o_ref = refs[2 * pages]
    qbd_ref, m_ref, l_ref, acc_ref = refs[2 * pages + 1:]
    j = pl.program_id(1)

    @pl.when(j == 0)
    def _init():
        q = jnp.broadcast_to(q_ref[0] * (ATTN_SCALE * LOG2E), (DEC_ROWS, A_QK))
        row = lax.broadcasted_iota(jnp.int32, q.shape, 0)
        blk = lax.broadcasted_iota(jnp.int32, q.shape, 1) // HEAD_DIM
        qbd_ref[...] = jnp.where(row == blk, q, 0.0)
        m_ref[...] = jnp.full(m_ref.shape, -jnp.inf, F32)
        l_ref[...] = jnp.zeros(l_ref.shape, F32)
        acc_ref[...] = jnp.zeros(acc_ref.shape, F32)

    qbd = qbd_ref[...].astype(BF16)
    s = jnp.concatenate(
        [jnp.dot(qbd, kp[p][...].astype(BF16), preferred_element_type=F32)
         for p in range(pages)], axis=1)
    m_prev = m_ref[...]
    m_new = jnp.maximum(m_prev, jnp.max(s, axis=-1, keepdims=True))
    alpha = jnp.exp2(m_prev - m_new)
    p_all = jnp.exp2(s - _tile_lanes(m_new, pages))
    l_ref[...] = alpha * l_ref[...] + jnp.sum(p_all, axis=-1, keepdims=True)
    p_bf = p_all.astype(BF16)
    for h in range(A_HEADS):
        pv = jnp.zeros((DEC_ROWS, A_VDIM), F32)
        for p in range(pages):
            v_h = vp[p][pl.ds(h, PAGE_SIZE, stride=A_HEADS), :]
            pv = pv + jnp.dot(p_bf[:, p * PAGE_SIZE:(p + 1) * PAGE_SIZE],
                              v_h.astype(BF16), preferred_element_type=F32)
        acc_ref[h] = alpha * acc_ref[h] + pv
    m_ref[...] = m_new

    @pl.when(j == pl.num_programs(1) - 1)
    def _finish():
        s_cur = jnp.sum(qbd_ref[...] * kc_ref[0], axis=-1, keepdims=True)
        m_old = m_ref[...]
        m_fin = jnp.maximum(m_old, s_cur)
        a_fin = jnp.exp2(m_old - m_fin)
        p_cur = jnp.exp2(s_cur - m_fin)
        l_fin = a_fin * l_ref[...] + p_cur
        lam = lam_ref[0]
        sw = sw_ref[...]
        vc = vc_ref[0]
        for h in range(A_HEADS):
            cols = slice(h * A_VDIM, (h + 1) * A_VDIM)
            a = (a_fin * acc_ref[h] + p_cur * vc[:, cols]) / l_fin
            o = a[2 * h:2 * h + 1, :] - lam * a[2 * h + 1:2 * h + 2, :]
            o_ref[0, :, cols] = _rms(o, sw) * lam_scale


def _decode_attn(page_table, lam, qd, kd, vd, sw, cache_kt, cache_vf, e, lam_scale,
                 pages):
    nb, n_pages = page_table.shape
    kern = functools.partial(_decode_attn_kernel, pages=pages, lam_scale=lam_scale)
    tok = lambda b, j, pt: (b, 0, 0)

    def page_spec(p):
        return pl.BlockSpec((None, None, A_QK, PAGE_SIZE),
                            lambda b, j, pt: (e, pt[b, j * pages + p], 0, 0))

    grid_spec = pltpu.PrefetchScalarGridSpec(
        num_scalar_prefetch=1,
        grid=(nb, n_pages // pages),
        in_specs=([pl.BlockSpec(memory_space=pltpu.SMEM),
                   pl.BlockSpec((1, 1, A_QK), tok),
                   pl.BlockSpec((1, 1, A_QK), tok),
                   pl.BlockSpec((1, 1, A_V), tok),
                   pl.BlockSpec((1, LANES), lambda b, j, pt: (0, 0))]
                  + [page_spec(p) for p in range(pages)] * 2),
        out_specs=pl.BlockSpec((1, 1, A_V), tok),
        scratch_shapes=[pltpu.VMEM((DEC_ROWS, A_QK), F32),
                        pltpu.VMEM((DEC_ROWS, LANES), F32),
                        pltpu.VMEM((DEC_ROWS, LANES), F32),
                        pltpu.VMEM((A_HEADS, DEC_ROWS, A_VDIM), F32)])
    return pl.pallas_call(
        kern,
        grid_spec=grid_spec,
        out_shape=jax.ShapeDtypeStruct((nb, 1, A_V), F32),
        compiler_params=_cparams(("parallel", "arbitrary")),
        name="decode_diff_attn",
    )(page_table, lam, qd, kd, vd, sw, *([cache_kt] * pages), *([cache_vf] * pages))


def _ffn_rows(x, nw_in_ref, wi_ref, wo_ref, nw_out_ref):
    xn = _rms(x, nw_in_ref[...]).astype(BF16)
    y = jnp.zeros(x.shape, F32)
    for c in range(D_FF // FFN_CHUNK):
        lo = c * FFN_CHUNK
        gate = jnp.dot(xn, wi_ref[:, lo:lo + FFN_CHUNK], preferred_element_type=F32)
        up = jnp.dot(xn, wi_ref[:, D_FF + lo:D_FF + lo + FFN_CHUNK],
                     preferred_element_type=F32)
        act = (gate * _sigmoid(gate) * up).astype(BF16)
        y = y + jnp.dot(act, wo_ref[lo:lo + FFN_CHUNK, :], preferred_element_type=F32)
    return x + _rms(y, nw_out_ref[...])


def _even_tail_kernel(*refs, tm, seq_tiles, explicit_prev):
    if explicit_prev:
        o_ref, bg_ref, u_ref, um1_ref, um2_ref = refs[:5]
        rest = refs[5:]
        u = u_ref[...]
        um1 = um1_ref[...]
        um2 = um2_ref[...]
    else:
        o_ref, bg_ref, u_ref, halo_ref = refs[:4]
        rest = refs[4:]
        u = u_ref[...]
        halo = halo_ref[...]
        halo = jnp.where(pl.program_id(0) % seq_tiles == 0, 0.0, halo)
        row = lax.broadcasted_iota(jnp.int32, u.shape, 0)
        h1 = halo[SUBLANES - 1:SUBLANES, :]
        h2 = halo[SUBLANES - 2:SUBLANES - 1, :]
        um1 = jnp.where(row == 0, h1, pltpu.roll(u, 1, 0))
        um2 = jnp.where(row == 0, h2, jnp.where(row == 1, h1, pltpu.roll(u, 2, 0)))
    cw_ref, w_ref, x_ref, nw_ref, nw_in_ref, wi_ref, wo_ref, nw_out_ref, out_ref = rest
    cw = cw_ref[...]
    conv = um2 * cw[0:1, :] + um1 * cw[1:2, :] + u * cw[2:3, :]
    yb = bg_ref[...] * conv
    y = (jnp.dot(o_ref[...].astype(BF16), w_ref[0:A_V, :], preferred_element_type=F32)
         + jnp.dot(yb.astype(BF16), w_ref[A_V:A_V + B_WIDTH, :],
                   preferred_element_type=F32))
    x_mid = x_ref[...] + _rms(y, nw_ref[...])
    out_ref[...] = _ffn_rows(x_mid, nw_in_ref, wi_ref, wo_ref, nw_out_ref)


def _even_tail(o, bg, u, prev, cw, w, li, x, nw, wi, wo, layer, tm, seq_len):
    n = x.shape[0]
    row = lambda i: (i, 0)
    half = pl.BlockSpec((tm, B_WIDTH), row)
    vec = _const_spec((1, D_MODEL))
    explicit_prev = isinstance(prev, tuple)
    if explicit_prev:
        prev_args = list(prev)
        prev_specs = [half, half]
    else:
        prev_args = [u]
        prev_specs = [pl.BlockSpec(
            (SUBLANES, B_WIDTH),
            lambda i: (jnp.maximum(i * (tm // SUBLANES) - 1, 0), 0))]
    kern = functools.partial(_even_tail_kernel, tm=tm, seq_tiles=seq_len // tm,
                             explicit_prev=explicit_prev)
    return pl.pallas_call(
        kern,
        grid=(n // tm,),
        in_specs=([half, half, half] + prev_specs
                  + [_const_spec((CONV_W, B_WIDTH)),
                     _layer_spec(w, li),
                     pl.BlockSpec((tm, D_MODEL), row),
                     vec, vec, _layer_spec(wi, layer), _layer_spec(wo, layer), vec]),
        out_specs=pl.BlockSpec((tm, D_MODEL), row),
        out_shape=jax.ShapeDtypeStruct((n, D_MODEL), F32),
        compiler_params=_cparams(("parallel",)),
        name="even_tail",
    )(o, bg, u, *prev_args, cw, w, x, nw[1], nw[2], wi, wo, nw[3])


def _odd_in_kernel(x_ref, nw_ref, w_ref, lb_ref, z_ref):
    xn = _rms(x_ref[...], nw_ref[...]).astype(BF16)

    def proj(c):
        return jnp.dot(xn, w_ref[:, c * D_MODEL:(c + 1) * D_MODEL],
                       preferred_element_type=F32)

    zq = proj(0)
    z_ref[:, 0:HG_FDIM] = zq * _sigmoid(zq)
    lb = lb_ref[...]
    z_ref[:, HG_FDIM:2 * HG_FDIM] = lb + (1.0 - lb) * _sigmoid(proj(1))
    z_ref[:, 2 * HG_FDIM:2 * HG_FDIM + D_MODEL] = proj(2)
    zg = proj(3)
    z_ref[:, 2 * HG_FDIM + D_MODEL:ODD_IN] = zg * _sigmoid(zg)


def _odd_in(x, nw, w, li, lb, tm):
    n = x.shape[0]
    row = lambda i: (i, 0)
    return pl.pallas_call(
        _odd_in_kernel,
        grid=(n // tm,),
        in_specs=[pl.BlockSpec((tm, D_MODEL), row),
                  _const_spec((1, D_MODEL)),
                  _layer_spec(w, li),
                  _const_spec((1, HG_FDIM))],
        out_specs=pl.BlockSpec((tm, ODD_IN), row),
        out_shape=jax.ShapeDtypeStruct((n, ODD_IN), F32),
        compiler_params=_cparams(("parallel",)),
        name="odd_in",
    )(x, nw, w, lb)


def _gla_tables():
    c = GLA_CHUNK
    t = np.arange(c)[:, None]
    r = np.arange(c)[None, :]
    mats = []
    for lev in range(GLA_SUM_LEVELS):
        half = 1 << lev
        mid = (t // (2 * half)) * (2 * half) + half
        mats.append(np.where(t >= mid, (r >= mid) & (r <= t), (r > t) & (r < mid)))
    mats.append(r <= t)
    sums = np.concatenate(mats, axis=0).astype(np.float32)
    high_bit = np.floor(np.log2(np.maximum(t ^ r, 1))).astype(np.int32)
    level = np.where(t > r, high_bit, np.where(t == r, GLA_LEVELS, GLA_LEVELS + 1))
    sign = np.stack([np.where((t >> lev) & 1 == 1, 1.0, -1.0) * np.ones((1, HG_DK))
                     for lev in range(GLA_SUM_LEVELS, GLA_LEVELS)]).astype(np.float32)
    return (jnp.asarray(sums, BF16), jnp.asarray(level.astype(np.int32)),
            jnp.asarray(sign))


def _gla_kernel(q_ref, f_ref, v_ref, g_ref, gw_ref, sums_ref, level_ref, sign_ref,
                og_ref, sout_ref, st_ref, o_s, *, rows):
    c = GLA_CHUNK
    tb = pl.program_id(2)

    @pl.when(tb == 0)
    def _zero_state():
        st_ref[...] = jnp.zeros(st_ref.shape, F32)

    q_all = q_ref[...]
    fg = f_ref[...]
    logf_all = jnp.log(fg) * LOG2E
    k_all = 1.0 - fg
    v_all = v_ref[...]
    level = level_ref[...]
    row_id = lax.broadcasted_iota(jnp.int32, (c, HG_DK), 0)
    contract_last = (((1,), (1,)), ((), ()))

    st = st_ref[...]
    for ch in range(rows // c):
        sl = slice(ch * c, (ch + 1) * c)
        q = q_all[sl]
        k = k_all[sl]
        v = v_all[sl].astype(BF16)
        logf = logf_all[sl]
        hi = logf.astype(BF16)
        lo = (logf - hi.astype(F32)).astype(BF16)
        w = jnp.dot(sums_ref[...], jnp.concatenate([hi, lo], axis=1),
                    preferred_element_type=F32)
        w = w[:, 0:HG_DK] + w[:, HG_DK:2 * HG_DK]
        cum = w[GLA_SUM_LEVELS * c:(GLA_SUM_LEVELS + 1) * c]
        a = jnp.zeros((c, c), F32)
        for lev in range(GLA_LEVELS):
            if lev < GLA_SUM_LEVELS:
                expo = w[lev * c:(lev + 1) * c]
            else:
                half = 1 << lev
                at_split = jnp.concatenate(
                    [jnp.broadcast_to(cum[b0 + half - 1:b0 + half, :], (2 * half, HG_DK))
                     for b0 in range(0, c, 2 * half)], axis=0)
                expo = (cum - at_split) * sign_ref[lev - GLA_SUM_LEVELS]
            upper = ((row_id >> lev) & 1) == 1
            y = (jnp.where(upper, q, k) * jnp.exp2(expo)).astype(BF16)
            a = jnp.where(level == lev,
                          lax.dot_general(y, y, contract_last,
                                          preferred_element_type=F32), a)
        a = jnp.where(level == GLA_LEVELS,
                      lax.dot_general(q.astype(BF16), k.astype(BF16), contract_last,
                                      preferred_element_type=F32), a)
        o_intra = jnp.dot(a.astype(BF16), v, preferred_element_type=F32)
        total = cum[c - 1:c, :]
        qh = (q * jnp.exp2(cum)).astype(BF16)
        kh = (k * jnp.exp2(total - cum)).astype(BF16)
        o_inter = lax.dot_general(qh, st.astype(BF16), contract_last,
                                  preferred_element_type=F32)
        st = st * jnp.exp2(total) + lax.dot_general(v, kh, (((0,), (0,)), ((), ())),
                                                    preferred_element_type=F32)
        o_s[sl, :] = o_inter + o_intra
    st_ref[...] = st
    og_ref[...] = _rms(o_s[...], gw_ref[...]) * g_ref[...]

    @pl.when(tb == pl.num_programs(2) - 1)
    def _store_state():
        sout_ref[0, 0] = st.T


def _gla(z, gw, nseq, t, rows):
    nt = t // rows
    sums, level, sign = _gla_tables()

    def col_spec(base):
        return pl.BlockSpec((rows, HG_DK),
                            lambda s, h, tb: (s * nt + tb, base + h))

    kern = functools.partial(_gla_kernel, rows=rows)
    return pl.pallas_call(
        kern,
        grid=(nseq, HG_HEADS, nt),
        in_specs=[col_spec(0), col_spec(HG_HEADS), col_spec(2 * HG_HEADS),
                  col_spec(3 * HG_HEADS),
                  pl.BlockSpec((1, HG_DV), lambda s, h, tb: (0, 0)),
                  _const_spec(sums.shape), _const_spec(level.shape),
                  _const_spec(sign.shape)],
        out_specs=[pl.BlockSpec((rows, HG_DV), lambda s, h, tb: (s * nt + tb, h)),
                   pl.BlockSpec((1, 1, HG_DK, HG_DV), lambda s, h, tb: (s, h, 0, 0))],
        out_shape=[jax.ShapeDtypeStruct((nseq * t, D_MODEL), F32),
                   jax.ShapeDtypeStruct((nseq, HG_HEADS, HG_DK, HG_DV), F32)],
        scratch_shapes=[pltpu.VMEM((HG_DV, HG_DK), F32),
                        pltpu.VMEM((rows, HG_DV), F32)],
        compiler_params=_cparams(("parallel", "parallel", "arbitrary")),
        name="hgrn2_gla",
    )(z, z, z, z, gw, sums, level, sign)


def _gla_decode_kernel(z_ref, gw_ref, s0_ref, og_ref, sout_ref):
    z = z_ref[0]
    eye = (lax.broadcasted_iota(jnp.int32, (HG_DK, HG_DK), 0)
           == lax.broadcasted_iota(jnp.int32, (HG_DK, HG_DK), 1))

    def column(row_vec):
        return jnp.sum(jnp.where(eye, row_vec, 0.0), axis=-1, keepdims=True)

    gw = gw_ref[...]
    for h in range(HG_HEADS):
        cols = slice(h * HG_DK, (h + 1) * HG_DK)
        q = z[:, cols]
        fg = z[:, HG_FDIM + h * HG_DK:HG_FDIM + (h + 1) * HG_DK]
        v = z[:, 2 * HG_FDIM + h * HG_DV:2 * HG_FDIM + (h + 1) * HG_DV]
        gate = z[:, 2 * HG_FDIM + D_MODEL + h * HG_DV:2 * HG_FDIM + D_MODEL + (h + 1) * HG_DV]
        s_new = column(fg) * s0_ref[h] + column(1.0 - fg) * v
        sout_ref[0, h] = s_new
        o = jnp.sum(column(q) * s_new, axis=0, keepdims=True)
        og_ref[0, :, cols] = _rms(o, gw) * gate


def _gla_decode(z, gw, states, li):
    nseq = z.shape[0]
    tok = lambda s: (s, 0, 0)
    heads = (HG_HEADS, HG_DK, HG_DV)
    return pl.pallas_call(
        _gla_decode_kernel,
        grid=(nseq,),
        in_specs=[pl.BlockSpec((1, 1, ODD_IN), tok),
                  _const_spec((1, HG_DV)),
                  pl.BlockSpec((None, None) + heads, lambda s: (li, s, 0, 0, 0))],
        out_specs=[pl.BlockSpec((1, 1, D_MODEL), tok),
                   pl.BlockSpec((1,) + heads, lambda s: (s, 0, 0, 0))],
        out_shape=[jax.ShapeDtypeStruct((nseq, 1, D_MODEL), F32),
                   jax.ShapeDtypeStruct((nseq,) + heads, F32)],
        compiler_params=_cparams(("parallel",)),
        name="hgrn2_decode",
    )(z.reshape(nseq, 1, ODD_IN), gw, states)


def _odd_tail_kernel(a_ref, w_ref, x_ref, nw_ref, nw_in_ref, wi_ref, wo_ref, nw_out_ref,
                     out_ref):
    y = jnp.dot(a_ref[...].astype(BF16), w_ref[...], preferred_element_type=F32)
    x_mid = x_ref[...] + _rms(y, nw_ref[...])
    out_ref[...] = _ffn_rows(x_mid, nw_in_ref, wi_ref, wo_ref, nw_out_ref)


def _odd_tail(a, w, li, x, nw, wi, wo, layer, tm):
    n = x.shape[0]
    row = lambda i: (i, 0)
    vec = _const_spec((1, D_MODEL))
    return pl.pallas_call(
        _odd_tail_kernel,
        grid=(n // tm,),
        in_specs=[pl.BlockSpec((tm, D_MODEL), row),
                  _layer_spec(w, li),
                  pl.BlockSpec((tm, D_MODEL), row),
                  vec, vec, _layer_spec(wi, layer), _layer_spec(wo, layer), vec],
        out_specs=pl.BlockSpec((tm, D_MODEL), row),
        out_shape=jax.ShapeDtypeStruct((n, D_MODEL), F32),
        compiler_params=_cparams(("parallel",)),
        name="odd_tail",
    )(a, w, x, nw[1], nw[2], wi, wo, nw[3])


def _rope_tables(pos):
    half = HEAD_DIM // 2
    inv = ROPE_THETA ** (-jnp.arange(half, dtype=F32) / half)
    ang = pos.astype(F32)[:, None] * inv[None, :]
    cos = jnp.tile(jnp.cos(ang), (1, LANES // half))
    sin = jnp.tile(jnp.concatenate([-jnp.sin(ang), jnp.sin(ang)], axis=1),
                   (1, LANES // HEAD_DIM))
    return cos, sin


def kernel(x_prompt, x_sample, cache_k, cache_v, state_conv, state_hgrn, page_table,
           norm_w, w_in_even, w_out_even, conv_w, lambda_q1, lambda_k1, lambda_q2,
           lambda_k2, subln_w, w_in_odd, w_out_odd, hgrn_lb, gnorm_w, w_ffn_in,
           w_ffn_out):
    nb, t, _ = x_prompt.shape
    nd, td, _ = x_sample.shape
    assert td == 1
    depth = norm_w.shape[0]
    n_pages = page_table.shape[1]
    past_len = n_pages * PAGE_SIZE
    tm_p = 512
    tm_d = nd
    dec_pages = 32
    gla_rows = 2048
    flash_tile = 1024

    cos_p, sin_p = _rope_tables(jnp.arange(t))
    cos_d, sin_d = _rope_tables(jnp.full((nd,), past_len))
    p_lb = jax.nn.softmax(hgrn_lb.astype(F32), axis=0)
    lower_bounds = jnp.cumsum(p_lb, axis=0) - p_lb[0]

    n_even, n_phys = cache_k.shape[:2]
    ck = jnp.transpose(cache_k, (0, 1, 3, 4, 5, 2)).reshape(n_even, n_phys, A_QK, PAGE_SIZE)
    cv = cache_v.reshape(n_even, n_phys, PAGE_SIZE * A_HEADS, A_VDIM)
    nw = norm_w.reshape(depth, 4, 1, D_MODEL)
    w_in_e, w_out_e = w_in_even.astype(BF16), w_out_even.astype(BF16)
    w_in_o, w_out_o = w_in_odd.astype(BF16), w_out_odd.astype(BF16)
    w_ffn_i, w_ffn_o = w_ffn_in.astype(BF16), w_ffn_out.astype(BF16)

    xp = x_prompt.reshape(nb * t, D_MODEL)
    xs = x_sample.reshape(nd, D_MODEL)
    kp_l, vp_l, cp_l, hp_l = [], [], [], []
    ks_l, vs_l, cs_l, hs_l = [], [], [], []
    for layer in range(depth):
        if layer % 2 == 0:
            e = layer // 2
            lam_init = 0.8 - 0.6 * math.exp(-0.3 * layer)
            lam = (jnp.exp(jnp.sum(lambda_q1[e].astype(F32) * lambda_k1[e].astype(F32)))
                   - jnp.exp(jnp.sum(lambda_q2[e].astype(F32) * lambda_k2[e].astype(F32)))
                   + lam_init).reshape(1)
            sw = subln_w[e].reshape(1, A_VDIM)
            cw = conv_w[e]

            q, k, v, bg, u = _even_in(xp, nw[layer, 0], w_in_e, e, cos_p, sin_p, tm_p)
            o = _flash(lam, q, k, v, sw, nb, t, 1.0 - lam_init, flash_tile)
            xp = _even_tail(o, bg, u, None, cw, w_out_e, e, xp, nw[layer], w_ffn_i, w_ffn_o,
                            layer, tm_p, t)

            qd, kd, vd, bgd, ud = _even_in(xs, nw[layer, 0], w_in_e, e, cos_d, sin_d, tm_d)
            od = _decode_attn(page_table, lam, qd.reshape(nd, 1, A_QK),
                              kd.reshape(nd, 1, A_QK), vd.reshape(nd, 1, A_V), sw,
                              ck, cv, e, 1.0 - lam_init, dec_pages)
            sc = state_conv[e]
            xs = _even_tail(od.reshape(nd, A_V), bgd, ud, (sc[:, 1], sc[:, 0]), cw,
                            w_out_e, e, xs, nw[layer], w_ffn_i, w_ffn_o, layer, tm_d, td)

            kp_l.append(k.reshape(nb, t, A_HEADS, 2, HEAD_DIM))
            vp_l.append(v.reshape(nb, t, A_HEADS, A_VDIM))
            cp_l.append(u.reshape(nb, t, B_WIDTH)[:, t - (CONV_W - 1):])
            ks_l.append(kd.reshape(nd, td, A_HEADS, 2, HEAD_DIM))
            vs_l.append(vd.reshape(nd, td, A_HEADS, A_VDIM))
            cs_l.append(jnp.concatenate([sc[:, 1:], ud[:, None, :]], axis=1))
        else:
            o_idx = layer // 2
            lb = lower_bounds[layer].reshape(1, HG_FDIM)
            gw = gnorm_w[o_idx].reshape(1, HG_DV)

            z = _odd_in(xp, nw[layer, 0], w_in_o, o_idx, lb, tm_p)
            og, sp = _gla(z, gw, nb, t, gla_rows)
            xp = _odd_tail(og, w_out_o, o_idx, xp, nw[layer], w_ffn_i, w_ffn_o, layer, tm_p)

            zd = _odd_in(xs, nw[layer, 0], w_in_o, o_idx, lb, tm_d)
            ogd, ss = _gla_decode(zd, gw, state_hgrn, o_idx)
            xs = _odd_tail(ogd.reshape(nd, D_MODEL), w_out_o, o_idx, xs, nw[layer], w_ffn_i,
                           w_ffn_o, layer, tm_d)
            hp_l.append(sp)
            hs_l.append(ss)

    return (xp.reshape(nb, t, D_MODEL), xs.reshape(nd, td, D_MODEL),
            jnp.stack(kp_l), jnp.stack(vp_l), jnp.stack(cp_l), jnp.stack(hp_l),
            jnp.stack(ks_l), jnp.stack(vs_l), jnp.stack(cs_l), jnp.stack(hs_l))
```

```python
import functools
import math

import numpy as np
import jax
import jax.numpy as jnp
from jax import lax
from jax.experimental import pallas as pl
from jax.experimental.pallas import tpu as pltpu

D_MODEL = 1024
PAGE_SIZE = 128
HEAD_DIM = 64
A_HEADS = 4
A_QK = A_HEADS * 2 * HEAD_DIM
A_VDIM = 2 * HEAD_DIM
A_V = A_HEADS * A_VDIM
ATTN_SCALE = HEAD_DIM ** -0.5
ROPE_THETA = 10000.0
B_WIDTH = D_MODEL // 2
CONV_W = 3
EVEN_IN = 2 * A_QK + A_V + 3 * B_WIDTH
HG_HEADS = 8
HG_DK = 128
HG_FDIM = HG_HEADS * HG_DK
HG_DV = D_MODEL // HG_HEADS
ODD_IN = 2 * HG_FDIM + 2 * D_MODEL
D_FF = -(-8 * D_MODEL // (3 * 256)) * 256
EPS = 1e-6
LOG2E = math.log2(math.e)

LANES = 128
SUBLANES = 8
VMEM_LIMIT_BYTES = 56 * 1024 * 1024
GLA_CHUNK = 128
GLA_LEVELS = 7
GLA_SUM_LEVELS = 3
FFN_CHUNK = 256
FLASH_ROWS = 128

F32 = jnp.float32
BF16 = jnp.bfloat16


def _cparams(semantics):
    return pltpu.CompilerParams(dimension_semantics=semantics,
                                vmem_limit_bytes=VMEM_LIMIT_BYTES)


def _rms(x, w):
    return x * lax.rsqrt(jnp.mean(x * x, axis=-1, keepdims=True) + EPS) * w


def _sigmoid(x):
    return 1.0 / (1.0 + jnp.exp(-x))


def _tile_lanes(x, n):
    return x if n == 1 else jnp.concatenate([x] * n, axis=1)


def _layer_spec(stacked, li):
    shape = stacked.shape[1:]
    nd = len(shape)
    return pl.BlockSpec((None,) + shape, lambda *_: (li,) + (0,) * nd,
                        pipeline_mode=pl.Buffered(1))


def _const_spec(shape):
    nd = len(shape)
    return pl.BlockSpec(shape, lambda *_: (0,) * nd, pipeline_mode=pl.Buffered(1))


def _even_in_kernel(x_ref, nw_ref, w_ref, cos_ref, sin_ref,
                    q_ref, k_ref, v_ref, bg_ref, u_ref):
    xn = _rms(x_ref[...], nw_ref[...]).astype(BF16)
    cos = cos_ref[...]
    sin = sin_ref[...]
    lane = lax.broadcasted_iota(jnp.int32, cos.shape, 1)
    first_half = (lane % HEAD_DIM) < (HEAD_DIM // 2)

    def proj(c):
        return jnp.dot(xn, w_ref[:, c * A_QK:(c + 1) * A_QK],
                       preferred_element_type=F32)

    def rope(z, out_ref):
        for s in range(A_QK // LANES):
            zs = z[:, s * LANES:(s + 1) * LANES]
            swapped = jnp.where(first_half,
                                pltpu.roll(zs, LANES - HEAD_DIM // 2, 1),
                                pltpu.roll(zs, HEAD_DIM // 2, 1))
            out_ref[:, s * LANES:(s + 1) * LANES] = zs * cos + swapped * sin

    rope(proj(0), q_ref)
    rope(proj(1), k_ref)
    v_ref[...] = proj(2)
    bg_ref[...] = proj(3)
    u_ref[...] = proj(4) * proj(5)


def _even_in(x, nw, w, li, cos, sin, tm):
    n = x.shape[0]
    row = lambda i: (i, 0)
    pos_tiles = cos.shape[0] // tm
    pos = lambda i: (i % pos_tiles, 0)
    out = jax.ShapeDtypeStruct((n, A_QK), F32)
    return pl.pallas_call(
        _even_in_kernel,
        grid=(n // tm,),
        in_specs=[pl.BlockSpec((tm, D_MODEL), row),
                  _const_spec((1, D_MODEL)),
                  _layer_spec(w, li),
                  pl.BlockSpec((tm, LANES), pos),
                  pl.BlockSpec((tm, LANES), pos)],
        out_specs=[pl.BlockSpec((tm, A_QK), row)] * 5,
        out_shape=[out] * 5,
        compiler_params=_cparams(("parallel",)),
        name="even_in",
    )(x, nw, w, cos, sin)


def _flash_kernel(ii_ref, jj_ref, lam_ref, q_ref, k_ref, v_ref, sw_ref, o_ref,
                  qs_ref, m_ref, acc_ref, *, tq, tk, lam_scale):
    pair = pl.program_id(2)
    j = jj_ref[pair]
    d = j - ii_ref[pair] * (tq // tk)
    per_part = tk // FLASH_ROWS

    @pl.when(j == 0)
    def _init():
        qs_ref[...] = (q_ref[...] * (ATTN_SCALE * LOG2E)).astype(BF16)
        m_ref[...] = jnp.full(m_ref.shape, -jnp.inf, F32)
        acc_ref[...] = jnp.zeros(acc_ref.shape, F32)

    def step(diag):
        kf = k_ref[...]
        lane = lax.broadcasted_iota(jnp.int32, kf.shape, 1)
        v = v_ref[...].astype(BF16)
        v_ext = jnp.concatenate([v, jnp.ones_like(v)], axis=1)
        tri = (lax.broadcasted_iota(jnp.int32, (FLASH_ROWS, FLASH_ROWS), 1)
               <= lax.broadcasted_iota(jnp.int32, (FLASH_ROWS, FLASH_ROWS), 0))
        for mp in range(2):
            keep = (lane < HEAD_DIM) if mp == 0 else (lane >= HEAD_DIM)
            k = jnp.where(keep, kf, 0.0).astype(BF16)
            for rc in range(tq // FLASH_ROWS):
                part, local = divmod(rc, per_part)
                if diag is not None and part < diag:
                    continue
                on_diag = diag is not None and part == diag
                rows = slice(rc * FLASH_ROWS, (rc + 1) * FLASH_ROWS)
                first = on_diag and (local + 1) * FLASH_ROWS <= tk // 2
                ncol = tk // 2 if first else tk
                s = lax.dot_general(qs_ref[rows, :], k[0:ncol], (((1,), (1,)), ((), ())),
                                    preferred_element_type=F32)
                if on_diag:
                    lo = local * FLASH_ROWS
                    hi = lo + FLASH_ROWS
                    parts = [s[:, 0:lo]] if local > 0 else []
                    parts.append(jnp.where(tri, s[:, lo:hi], -jnp.inf))
                    if hi < ncol:
                        parts.append(jnp.full((FLASH_ROWS, ncol - hi), -jnp.inf, F32))
                    s = jnp.concatenate(parts, axis=1)
                m_prev = m_ref[mp, rows, :]
                m_new = jnp.maximum(m_prev, jnp.max(s, axis=-1, keepdims=True))
                alpha = jnp.exp2(m_prev - m_new)
                p = jnp.exp2(s - _tile_lanes(m_new, ncol // LANES))
                pv = jnp.dot(p.astype(BF16), v_ext[0:ncol], preferred_element_type=F32)
                acc_ref[mp, rows, :] = _tile_lanes(alpha, 2) * acc_ref[mp, rows, :] + pv
                m_ref[mp, rows, :] = m_new

    @pl.when(d < 0)
    def _visible():
        step(None)

    for diag in range(tq // tk):
        @pl.when(d == diag)
        def _diagonal(diag=diag):
            step(diag)
            if diag == tq // tk - 1:
                a0 = acc_ref[0]
                a1 = acc_ref[1]
                o = (a0[:, 0:LANES] / a0[:, LANES:2 * LANES]
                     - lam_ref[0] * (a1[:, 0:LANES] / a1[:, LANES:2 * LANES]))
                o_ref[...] = _rms(o, sw_ref[...]) * lam_scale


def _flash(lam, q, k, v, sw, nb, t, lam_scale, tq, tk):
    nq, nk, ratio = t // tq, t // tk, tq // tk
    pairs = [(i, j) for i in range(nq) for j in range(ratio * (i + 1))]
    ii = jnp.asarray(np.array([p[0] for p in pairs], np.int32))
    jj = jnp.asarray(np.array([p[1] for p in pairs], np.int32))
    kern = functools.partial(_flash_kernel, tq=tq, tk=tk, lam_scale=lam_scale)
    qmap = lambda b, h, p, ii, jj: (b * nq + ii[p], h)
    kmap = lambda b, h, p, ii, jj: (b * nk + jj[p], h)
    grid_spec = pltpu.PrefetchScalarGridSpec(
        num_scalar_prefetch=2,
        grid=(nb, A_HEADS, len(pairs)),
        in_specs=[pl.BlockSpec(memory_space=pltpu.SMEM),
                  pl.BlockSpec((tq, LANES), qmap),
                  pl.BlockSpec((tk, LANES), kmap),
                  pl.BlockSpec((tk, LANES), kmap),
                  pl.BlockSpec((1, LANES), lambda b, h, p, ii, jj: (0, 0))],
        out_specs=pl.BlockSpec((tq, LANES), qmap),
        scratch_shapes=[pltpu.VMEM((tq, LANES), BF16),
                        pltpu.VMEM((2, tq, LANES), F32),
                        pltpu.VMEM((2, tq, 2 * LANES), F32)])
    return pl.pallas_call(
        kern,
        grid_spec=grid_spec,
        out_shape=jax.ShapeDtypeStruct((nb * t, A_V), F32),
        compiler_params=_cparams(("parallel", "parallel", "arbitrary")),
        name="flash_diff_attn",
    )(ii, jj, lam, q, k, v, sw)


DEC_ROWS = 16


def _decode_attn_kernel(pt_ref, lam_ref, q_ref, kc_ref, vc_ref, sw_ref, *refs,
                        pages, lam_scale):
    del pt_ref
    kp = refs[0:pages]
    vp = refs[pages:2 * pages]
    o_ref = refs[2 * pages]
    qbd_ref, m_ref, l_ref, acc_ref = refs[2 * pages + 1:]
    j = pl.program_id(1)

    @pl.when(j == 0)
    def _init():
        q = jnp.broadcast_to(q_ref[0] * (ATTN_SCALE * LOG2E), (DEC_ROWS, A_QK))
        row = lax.broadcasted_iota(jnp.int32, q.shape, 0)
        blk = lax.broadcasted_iota(jnp.int32, q.shape, 1) // HEAD_DIM
        qbd_ref[...] = jnp.where(row == blk, q, 0.0)
        m_ref[...] = jnp.full(m_ref.shape, -jnp.inf, F32)
        l_ref[...] = jnp.zeros(l_ref.shape, F32)
        acc_ref[...] = jnp.zeros(acc_ref.shape, F32)

    qbd = qbd_ref[...].astype(BF16)
    s = jnp.concatenate(
        [jnp.dot(qbd, kp[p][...].astype(BF16), preferred_element_type=F32)
         for p in range(pages)], axis=1)
    m_prev = m_ref[...]
    m_new = jnp.maximum(m_prev, jnp.max(s, axis=-1, keepdims=True))
    alpha = jnp.exp2(m_prev - m_new)
    p_all = jnp.exp2(s - _tile_lanes(m_new, pages))
    l_ref[...] = alpha * l_ref[...] + jnp.sum(p_all, axis=-1, keepdims=True)
    p_bf = p_all.astype(BF16)
    for h in range(A_HEADS):
        pv = jnp.zeros((DEC_ROWS, A_VDIM), F32)
        for p in range(pages):
            v_h = vp[p][pl.ds(h, PAGE_SIZE, stride=A_HEADS), :]
            pv = pv + jnp.dot(p_bf[:, p * PAGE_SIZE:(p + 1) * PAGE_SIZE],
                              v_h.astype(BF16), preferred_element_type=F32)
        acc_ref[h] = alpha * acc_ref[h] + pv
    m_ref[...] = m_new

    @pl.when(j == pl.num_programs(1) - 1)
    def _finish():
        s_cur = jnp.sum(qbd_ref[...] * kc_ref[0], axis=-1, keepdims=True)
        m_old = m_ref[...]
        m_fin = jnp.maximum(m_old, s_cur)
        a_fin = jnp.exp2(m_old - m_fin)
        p_cur = jnp.exp2(s_cur - m_fin)
        l_fin = a_fin * l_ref[...] + p_cur
        lam = lam_ref[0]
        sw = sw_ref[...]
        vc = vc_ref[0]
        for h in range(A_HEADS):
            cols = slice(h * A_VDIM, (h + 1) * A_VDIM)
            a = (a_fin * acc_ref[h] + p_cur * vc[:, cols]) / l_fin
            o = a[2 * h:2 * h + 1, :] - lam * a[2 * h + 1:2 * h + 2, :]
            o_ref[0, :, cols] = _rms(o, sw) * lam_scale


def _decode_attn(page_table, lam, qd, kd, vd, sw, cache_kt, cache_vf, e, lam_scale,
                 pages):
    nb, n_pages = page_table.shape
    kern = functools.partial(_decode_attn_kernel, pages=pages, lam_scale=lam_scale)
    tok = lambda b, j, pt: (b, 0, 0)

    def page_spec(p):
        return pl.BlockSpec((None, None, A_QK, PAGE_SIZE),
                            lambda b, j, pt: (e, pt[b, j * pages + p], 0, 0))

    grid_spec = pltpu.PrefetchScalarGridSpec(
        num_scalar_prefetch=1,
        grid=(nb, n_pages // pages),
        in_specs=([pl.BlockSpec(memory_space=pltpu.SMEM),
                   pl.BlockSpec((1, 1, A_QK), tok),
                   pl.BlockSpec((1, 1, A_QK), tok),
                   pl.BlockSpec((1, 1, A_V), tok),
                   pl.BlockSpec((1, LANES), lambda b, j, pt: (0, 0))]
                  + [page_spec(p) for p in range(pages)] * 2),
        out_specs=pl.BlockSpec((1, 1, A_V), tok),
        scratch_shapes=[pltpu.VMEM((DEC_ROWS, A_QK), F32),
                        pltpu.VMEM((DEC_ROWS, LANES), F32),
                        pltpu.VMEM((DEC_ROWS, LANES), F32),
                        pltpu.VMEM((A_HEADS, DEC_ROWS, A_VDIM), F32)])
    return pl.pallas_call(
        kern,
        grid_spec=grid_spec,
        out_shape=jax.ShapeDtypeStruct((nb, 1, A_V), F32),
        compiler_params=_cparams(("parallel", "arbitrary")),
        name="decode_diff_attn",
    )(page_table, lam, qd, kd, vd, sw, *([cache_kt] * pages), *([cache_vf] * pages))


def _ffn_rows(x, nw_in_ref, wi_ref, wo_ref, nw_out_ref):
    xn = _rms(x, nw_in_ref[...]).astype(BF16)
    y = jnp.zeros(x.shape, F32)
    for c in range(D_FF // FFN_CHUNK):
        lo = c * FFN_CHUNK
        gate = jnp.dot(xn, wi_ref[:, lo:lo + FFN_CHUNK], preferred_element_type=F32)
        up = jnp.dot(xn, wi_ref[:, D_FF + lo:D_FF + lo + FFN_CHUNK],
                     preferred_element_type=F32)
        act = (gate * _sigmoid(gate) * up).astype(BF16)
        y = y + jnp.dot(act, wo_ref[lo:lo + FFN_CHUNK, :], preferred_element_type=F32)
    return x + _rms(y, nw_out_ref[...])


def _even_tail_kernel(*refs, tm, seq_tiles, explicit_prev):
    if explicit_prev:
        o_ref, bg_ref, u_ref, um1_ref, um2_ref = refs[:5]
        rest = refs[5:]
        u = u_ref[...]
        um1 = um1_ref[...]
        um2 = um2_ref[...]
    else:
        o_ref, bg_ref, u_ref, halo_ref = refs[:4]
        rest = refs[4:]
        u = u_ref[...]
        halo = halo_ref[...]
        halo = jnp.where(pl.program_id(0) % seq_tiles == 0, 0.0, halo)
        row = lax.broadcasted_iota(jnp.int32, u.shape, 0)
        h1 = halo[SUBLANES - 1:SUBLANES, :]
        h2 = halo[SUBLANES - 2:SUBLANES - 1, :]
        um1 = jnp.where(row == 0, h1, pltpu.roll(u, 1, 0))
        um2 = jnp.where(row == 0, h2, jnp.where(row == 1, h1, pltpu.roll(u, 2, 0)))
    cw_ref, w_ref, x_ref, nw_ref, nw_in_ref, wi_ref, wo_ref, nw_out_ref, out_ref = rest
    cw = cw_ref[...]
    conv = um2 * cw[0:1, :] + um1 * cw[1:2, :] + u * cw[2:3, :]
    yb = bg_ref[...] * conv
    y = (jnp.dot(o_ref[...].astype(BF16), w_ref[0:A_V, :], preferred_element_type=F32)
         + jnp.dot(yb.astype(BF16), w_ref[A_V:A_V + B_WIDTH, :],
                   preferred_element_type=F32))
    x_mid = x_ref[...] + _rms(y, nw_ref[...])
    out_ref[...] = _ffn_rows(x_mid, nw_in_ref, wi_ref, wo_ref, nw_out_ref)


def _even_tail(o, bg, u, prev, cw, w, li, x, nw, wi, wo, layer, tm, seq_len):
    n = x.shape[0]
    row = lambda i: (i, 0)
    half = pl.BlockSpec((tm, B_WIDTH), row)
    vec = _const_spec((1, D_MODEL))
    explicit_prev = isinstance(prev, tuple)
    if explicit_prev:
        prev_args = list(prev)
        prev_specs = [half, half]
    else:
        prev_args = [u]
        prev_specs = [pl.BlockSpec(
            (SUBLANES, B_WIDTH),
            lambda i: (jnp.maximum(i * (tm // SUBLANES) - 1, 0), 0))]
    kern = functools.partial(_even_tail_kernel, tm=tm, seq_tiles=seq_len // tm,
                             explicit_prev=explicit_prev)
    return pl.pallas_call(
        kern,
        grid=(n // tm,),
        in_specs=([half, half, half] + prev_specs
                  + [_const_spec((CONV_W, B_WIDTH)),
                     _layer_spec(w, li),
                     pl.BlockSpec((tm, D_MODEL), row),
                     vec, vec, _layer_spec(wi, layer), _layer_spec(wo, layer), vec]),
        out_specs=pl.BlockSpec((tm, D_MODEL), row),
        out_shape=jax.ShapeDtypeStruct((n, D_MODEL), F32),
        compiler_params=_cparams(("parallel",)),
        name="even_tail",
    )(o, bg, u, *prev_args, cw, w, x, nw[1], nw[2], wi, wo, nw[3])


def _odd_in_kernel(x_ref, nw_ref, w_ref, lb_ref, z_ref):
    xn = _rms(x_ref[...], nw_ref[...]).astype(BF16)

    def proj(c):
        return jnp.dot(xn, w_ref[:, c * D_MODEL:(c + 1) * D_MODEL],
                       preferred_element_type=F32)

    zq = proj(0)
    z_ref[:, 0:HG_FDIM] = zq * _sigmoid(zq)
    lb = lb_ref[...]
    z_ref[:, HG_FDIM:2 * HG_FDIM] = lb + (1.0 - lb) * _sigmoid(proj(1))
    z_ref[:, 2 * HG_FDIM:2 * HG_FDIM + D_MODEL] = proj(2)
    zg = proj(3)
    z_ref[:, 2 * HG_FDIM + D_MODEL:ODD_IN] = zg * _sigmoid(zg)


def _odd_in(x, nw, w, li, lb, tm):
    n = x.shape[0]
    row = lambda i: (i, 0)
    return pl.pallas_call(
        _odd_in_kernel,
        grid=(n // tm,),
        in_specs=[pl.BlockSpec((tm, D_MODEL), row),
                  _const_spec((1, D_MODEL)),
                  _layer_spec(w, li),
                  _const_spec((1, HG_FDIM))],
        out_specs=pl.BlockSpec((tm, ODD_IN), row),
        out_shape=jax.ShapeDtypeStruct((n, ODD_IN), F32),
        compiler_params=_cparams(("parallel",)),
        name="odd_in",
    )(x, nw, w, lb)


def _gla_tables():
    c = GLA_CHUNK
    t = np.arange(c)[:, None]
    r = np.arange(c)[None, :]
    mats = []
    for lev in range(GLA_SUM_LEVELS):
        half = 1 << lev
        mid = (t // (2 * half)) * (2 * half) + half
        mats.append(np.where(t >= mid, (r >= mid) & (r <= t), (r > t) & (r < mid)))
    mats.append(r <= t)
    sums = np.concatenate(mats, axis=0).astype(np.float32)
    high_bit = np.floor(np.log2(np.maximum(t ^ r, 1))).astype(np.int32)
    level = np.where(t > r, high_bit, np.where(t == r, GLA_LEVELS, GLA_LEVELS + 1))
    sign = np.stack([np.where((t >> lev) & 1 == 1, 1.0, -1.0) * np.ones((1, HG_DK))
                     for lev in range(GLA_SUM_LEVELS, GLA_LEVELS)]).astype(np.float32)
    return (jnp.asarray(sums, BF16), jnp.asarray(level.astype(np.int32)),
            jnp.asarray(sign))


def _gla_kernel(q_ref, f_ref, v_ref, g_ref, gw_ref, sums_ref, level_ref, sign_ref,
                og_ref, sout_ref, st_ref, o_s, *, rows):
    c = GLA_CHUNK
    tb = pl.program_id(2)

    @pl.when(tb == 0)
    def _zero_state():
        st_ref[...] = jnp.zeros(st_ref.shape, F32)

    q_all = q_ref[...]
    fg = f_ref[...]
    logf_all = jnp.log(fg) * LOG2E
    k_all = 1.0 - fg
    v_all = v_ref[...]
    level = level_ref[...]
    row_id = lax.broadcasted_iota(jnp.int32, (c, HG_DK), 0)
    contract_last = (((1,), (1,)), ((), ()))

    st = st_ref[...]
    for ch in range(rows // c):
        sl = slice(ch * c, (ch + 1) * c)
        q = q_all[sl]
        k = k_all[sl]
        v = v_all[sl].astype(BF16)
        logf = logf_all[sl]
        hi = logf.astype(BF16)
        lo = (logf - hi.astype(F32)).astype(BF16)
        w = jnp.dot(sums_ref[...], jnp.concatenate([hi, lo], axis=1),
                    preferred_element_type=F32)
        w = w[:, 0:HG_DK] + w[:, HG_DK:2 * HG_DK]
        cum = w[GLA_SUM_LEVELS * c:(GLA_SUM_LEVELS + 1) * c]
        a = jnp.zeros((c, c), F32)
        for lev in range(GLA_LEVELS):
            if lev < GLA_SUM_LEVELS:
                expo = w[lev * c:(lev + 1) * c]
            else:
                half = 1 << lev
                at_split = jnp.concatenate(
                    [jnp.broadcast_to(cum[b0 + half - 1:b0 + half, :], (2 * half, HG_DK))
                     for b0 in range(0, c, 2 * half)], axis=0)
                expo = (cum - at_split) * sign_ref[lev - GLA_SUM_LEVELS]
            upper = ((row_id >> lev) & 1) == 1
            y = (jnp.where(upper, q, k) * jnp.exp2(expo)).astype(BF16)
            a = jnp.where(level == lev,
                          lax.dot_general(y, y, contract_last,
                                          preferred_element_type=F32), a)
        a = jnp.where(level == GLA_LEVELS,
                      lax.dot_general(q.astype(BF16), k.astype(BF16), contract_last,
                                      preferred_element_type=F32), a)
        o_intra = jnp.dot(a.astype(BF16), v, preferred_element_type=F32)
        total = cum[c - 1:c, :]
        qh = (q * jnp.exp2(cum)).astype(BF16)
        kh = (k * jnp.exp2(total - cum)).astype(BF16)
        o_inter = lax.dot_general(qh, st.astype(BF16), contract_last,
                                  preferred_element_type=F32)
        st = st * jnp.exp2(total) + lax.dot_general(v, kh, (((0,), (0,)), ((), ())),
                                                    preferred_element_type=F32)
        o_s[sl, :] = o_inter + o_intra
    st_ref[...] = st
    og_ref[...] = _rms(o_s[...], gw_ref[...]) * g_ref[...]

    @pl.when(tb == pl.num_programs(2) - 1)
    def _store_state():
        sout_ref[0, 0] = st.T


def _gla(z, gw, nseq, t, rows):
    nt = t // rows
    sums, level, sign = _gla_tables()

    def col_spec(base):
        return pl.BlockSpec((rows, HG_DK),
                            lambda s, h, tb: (s * nt + tb, base + h))

    kern = functools.partial(_gla_kernel, rows=rows)
    return pl.pallas_call(
        kern,
        grid=(nseq, HG_HEADS, nt),
        in_specs=[col_spec(0), col_spec(HG_HEADS), col_spec(2 * HG_HEADS),
                  col_spec(3 * HG_HEADS),
                  pl.BlockSpec((1, HG_DV), lambda s, h, tb: (0, 0)),
                  _const_spec(sums.shape), _const_spec(level.shape),
                  _const_spec(sign.shape)],
        out_specs=[pl.BlockSpec((rows, HG_DV), lambda s, h, tb: (s * nt + tb, h)),
                   pl.BlockSpec((1, 1, HG_DK, HG_DV), lambda s, h, tb: (s, h, 0, 0))],
        out_shape=[jax.ShapeDtypeStruct((nseq * t, D_MODEL), F32),
                   jax.ShapeDtypeStruct((nseq, HG_HEADS, HG_DK, HG_DV), F32)],
        scratch_shapes=[pltpu.VMEM((HG_DV, HG_DK), F32),
                        pltpu.VMEM((rows, HG_DV), F32)],
        compiler_params=_cparams(("parallel", "parallel", "arbitrary")),
        name="hgrn2_gla",
    )(z, z, z, z, gw, sums, level, sign)


def _gla_decode_kernel(z_ref, gw_ref, s0_ref, og_ref, sout_ref):
    z = z_ref[0]
    eye = (lax.broadcasted_iota(jnp.int32, (HG_DK, HG_DK), 0)
           == lax.broadcasted_iota(jnp.int32, (HG_DK, HG_DK), 1))

    def column(row_vec):
        return jnp.sum(jnp.where(eye, row_vec, 0.0), axis=-1, keepdims=True)

    gw = gw_ref[...]
    for h in range(HG_HEADS):
        cols = slice(h * HG_DK, (h + 1) * HG_DK)
        q = z[:, cols]
        fg = z[:, HG_FDIM + h * HG_DK:HG_FDIM + (h + 1) * HG_DK]
        v = z[:, 2 * HG_FDIM + h * HG_DV:2 * HG_FDIM + (h + 1) * HG_DV]
        gate = z[:, 2 * HG_FDIM + D_MODEL + h * HG_DV:2 * HG_FDIM + D_MODEL + (h + 1) * HG_DV]
        s_new = column(fg) * s0_ref[h] + column(1.0 - fg) * v
        sout_ref[0, h] = s_new
        o = jnp.sum(column(q) * s_new, axis=0, keepdims=True)
        og_ref[0, :, cols] = _rms(o, gw) * gate


def _gla_decode(z, gw, states, li):
    nseq = z.shape[0]
    tok = lambda s: (s, 0, 0)
    heads = (HG_HEADS, HG_DK, HG_DV)
    return pl.pallas_call(
        _gla_decode_kernel,
        grid=(nseq,),
        in_specs=[pl.BlockSpec((1, 1, ODD_IN), tok),
                  _const_spec((1, HG_DV)),
                  pl.BlockSpec((None, None) + heads, lambda s: (li, s, 0, 0, 0))],
        out_specs=[pl.BlockSpec((1, 1, D_MODEL), tok),
                   pl.BlockSpec((1,) + heads, lambda s: (s, 0, 0, 0))],
        out_shape=[jax.ShapeDtypeStruct((nseq, 1, D_MODEL), F32),
                   jax.ShapeDtypeStruct((nseq,) + heads, F32)],
        compiler_params=_cparams(("parallel",)),
        name="hgrn2_decode",
    )(z.reshape(nseq, 1, ODD_IN), gw, states)


def _odd_tail_kernel(a_ref, w_ref, x_ref, nw_ref, nw_in_ref, wi_ref, wo_ref, nw_out_ref,
                     out_ref):
    y = jnp.dot(a_ref[...].astype(BF16), w_ref[...], preferred_element_type=F32)
    x_mid = x_ref[...] + _rms(y, nw_ref[...])
    out_ref[...] = _ffn_rows(x_mid, nw_in_ref, wi_ref, wo_ref, nw_out_ref)


def _odd_tail(a, w, li, x, nw, wi, wo, layer, tm):
    n = x.shape[0]
    row = lambda i: (i, 0)
    vec = _const_spec((1, D_MODEL))
    return pl.pallas_call(
        _odd_tail_kernel,
        grid=(n // tm,),
        in_specs=[pl.BlockSpec((tm, D_MODEL), row),
                  _layer_spec(w, li),
                  pl.BlockSpec((tm, D_MODEL), row),
                  vec, vec, _layer_spec(wi, layer), _layer_spec(wo, layer), vec],
        out_specs=pl.BlockSpec((tm, D_MODEL), row),
        out_shape=jax.ShapeDtypeStruct((n, D_MODEL), F32),
        compiler_params=_cparams(("parallel",)),
        name="odd_tail",
    )(a, w, x, nw[1], nw[2], wi, wo, nw[3])


def _rope_tables(pos):
    half = HEAD_DIM // 2
    inv = ROPE_THETA ** (-jnp.arange(half, dtype=F32) / half)
    ang = pos.astype(F32)[:, None] * inv[None, :]
    cos = jnp.tile(jnp.cos(ang), (1, LANES // half))
    sin = jnp.tile(jnp.concatenate([-jnp.sin(ang), jnp.sin(ang)], axis=1),
                   (1, LANES // HEAD_DIM))
    return cos, sin


def kernel(x_prompt, x_sample, cache_k, cache_v, state_conv, state_hgrn, page_table,
           norm_w, w_in_even, w_out_even, conv_w, lambda_q1, lambda_k1, lambda_q2,
           lambda_k2, subln_w, w_in_odd, w_out_odd, hgrn_lb, gnorm_w, w_ffn_in,
           w_ffn_out):
    nb, t, _ = x_prompt.shape
    nd, td, _ = x_sample.shape
    assert td == 1
    depth = norm_w.shape[0]
    n_pages = page_table.shape[1]
    past_len = n_pages * PAGE_SIZE
    tm_p = 512
    tm_d = nd
    dec_pages = 32
    gla_rows = 2048
    flash_tq, flash_tk = 2048, 1024

    cos_p, sin_p = _rope_tables(jnp.arange(t))
    cos_d, sin_d = _rope_tables(jnp.full((nd,), past_len))
    p_lb = jax.nn.softmax(hgrn_lb.astype(F32), axis=0)
    lower_bounds = jnp.cumsum(p_lb, axis=0) - p_lb[0]

    n_even, n_phys = cache_k.shape[:2]
    ck = jnp.transpose(cache_k, (0, 1, 3, 4, 5, 2)).reshape(n_even, n_phys, A_QK, PAGE_SIZE)
    cv = cache_v.reshape(n_even, n_phys, PAGE_SIZE * A_HEADS, A_VDIM)
    nw = norm_w.reshape(depth, 4, 1, D_MODEL)
    w_in_e, w_out_e = w_in_even.astype(BF16), w_out_even.astype(BF16)
    w_in_o, w_out_o = w_in_odd.astype(BF16), w_out_odd.astype(BF16)
    w_ffn_i, w_ffn_o = w_ffn_in.astype(BF16), w_ffn_out.astype(BF16)

    xp = x_prompt.reshape(nb * t, D_MODEL)
    xs = x_sample.reshape(nd, D_MODEL)
    kp_l, vp_l, cp_l, hp_l = [], [], [], []
    ks_l, vs_l, cs_l, hs_l = [], [], [], []
    for layer in range(depth):
        if layer % 2 == 0:
            e = layer // 2
            lam_init = 0.8 - 0.6 * math.exp(-0.3 * layer)
            lam = (jnp.exp(jnp.sum(lambda_q1[e].astype(F32) * lambda_k1[e].astype(F32)))
                   - jnp.exp(jnp.sum(lambda_q2[e].astype(F32) * lambda_k2[e].astype(F32)))
                   + lam_init).reshape(1)
            sw = subln_w[e].reshape(1, A_VDIM)
            cw = conv_w[e]

            q, k, v, bg, u = _even_in(xp, nw[layer, 0], w_in_e, e, cos_p, sin_p, tm_p)
            o = _flash(lam, q, k, v, sw, nb, t, 1.0 - lam_init, flash_tq, flash_tk)
            xp = _even_tail(o, bg, u, None, cw, w_out_e, e, xp, nw[layer], w_ffn_i, w_ffn_o,
                            layer, tm_p, t)

            qd, kd, vd, bgd, ud = _even_in(xs, nw[layer, 0], w_in_e, e, cos_d, sin_d, tm_d)
            od = _decode_attn(page_table, lam, qd.reshape(nd, 1, A_QK),
                              kd.reshape(nd, 1, A_QK), vd.reshape(nd, 1, A_V), sw,
                              ck, cv, e, 1.0 - lam_init, dec_pages)
            sc = state_conv[e]
            xs = _even_tail(od.reshape(nd, A_V), bgd, ud, (sc[:, 1], sc[:, 0]), cw,
                            w_out_e, e, xs, nw[layer], w_ffn_i, w_ffn_o, layer, tm_d, td)

            kp_l.append(k.reshape(nb, t, A_HEADS, 2, HEAD_DIM))
            vp_l.append(v.reshape(nb, t, A_HEADS, A_VDIM))
            cp_l.append(u.reshape(nb, t, B_WIDTH)[:, t - (CONV_W - 1):])
            ks_l.append(kd.reshape(nd, td, A_HEADS, 2, HEAD_DIM))
            vs_l.append(vd.reshape(nd, td, A_HEADS, A_VDIM))
            cs_l.append(jnp.concatenate([sc[:, 1:], ud[:, None, :]], axis=1))
        else:
            o_idx = layer // 2
            lb = lower_bounds[layer].reshape(1, HG_FDIM)
            gw = gnorm_w[o_idx].reshape(1, HG_DV)

            z = _odd_in(xp, nw[layer, 0], w_in_o, o_idx, lb, tm_p)
            og, sp = _gla(z, gw, nb, t, gla_rows)
            xp = _odd_tail(og, w_out_o, o_idx, xp, nw[layer], w_ffn_i, w_ffn_o, layer, tm_p)

            zd = _odd_in(xs, nw[layer, 0], w_in_o, o_idx, lb, tm_d)
            ogd, ss = _gla_decode(zd, gw, state_hgrn, o_idx)
            xs = _odd_tail(ogd.reshape(nd, D_MODEL), w_out_o, o_idx, xs, nw[layer], w_ffn_i,
                           w_ffn_o, layer, tm_d)
            hp_l.append(sp)
            hs_l.append(ss)

    return (xp.reshape(nb, t, D_MODEL), xs.reshape(nd, td, D_MODEL),
            jnp.stack(kp_l), jnp.stack(vp_l), jnp.stack(cp_l), jnp.stack(hp_l),
            jnp.stack(ks_l), jnp.stack(vs_l), jnp.stack(cs_l), jnp.stack(hs_l))
```

```python
import functools
import math

import numpy as np
import jax
import jax.numpy as jnp
from jax import lax
from jax.experimental import pallas as pl
from jax.experimental.pallas import tpu as pltpu

D_MODEL = 1024
PAGE_SIZE = 128
HEAD_DIM = 64
A_HEADS = 4
A_QK = A_HEADS * 2 * HEAD_DIM
A_VDIM = 2 * HEAD_DIM
A_V = A_HEADS * A_VDIM
ATTN_SCALE = HEAD_DIM ** -0.5
ROPE_THETA = 10000.0
B_WIDTH = D_MODEL // 2
CONV_W = 3
EVEN_IN = 2 * A_QK + A_V + 3 * B_WIDTH
HG_HEADS = 8
HG_DK = 128
HG_FDIM = HG_HEADS * HG_DK
HG_DV = D_MODEL // HG_HEADS
ODD_IN = 2 * HG_FDIM + 2 * D_MODEL
D_FF = -(-8 * D_MODEL // (3 * 256)) * 256
EPS = 1e-6
LOG2E = math.log2(math.e)

LANES = 128
SUBLANES = 8
VMEM_LIMIT_BYTES = 56 * 1024 * 1024
GLA_CHUNK = 128
GLA_LEVELS = 7
GLA_SUM_LEVELS = 3
FFN_CHUNK = 256
FLASH_ROWS = 256

F32 = jnp.float32
BF16 = jnp.bfloat16


def _cparams(semantics):
    return pltpu.CompilerParams(dimension_semantics=semantics,
                                vmem_limit_bytes=VMEM_LIMIT_BYTES)


def _rms(x, w):
    return x * lax.rsqrt(jnp.mean(x * x, axis=-1, keepdims=True) + EPS) * w


def _sigmoid(x):
    return 1.0 / (1.0 + jnp.exp(-x))


def _tile_lanes(x, n):
    return x if n == 1 else jnp.concatenate([x] * n, axis=1)


def _layer_spec(stacked, li):
    shape = stacked.shape[1:]
    nd = len(shape)
    return pl.BlockSpec((None,) + shape, lambda *_: (li,) + (0,) * nd,
                        pipeline_mode=pl.Buffered(1))


def _const_spec(shape):
    nd = len(shape)
    return pl.BlockSpec(shape, lambda *_: (0,) * nd, pipeline_mode=pl.Buffered(1))


def _even_in_kernel(x_ref, nw_ref, w_ref, cos_ref, sin_ref,
                    q_ref, k_ref, v_ref, bg_ref, u_ref):
    xn = _rms(x_ref[...], nw_ref[...]).astype(BF16)
    cos = cos_ref[...]
    sin = sin_ref[...]
    lane = lax.broadcasted_iota(jnp.int32, cos.shape, 1)
    first_half = (lane % HEAD_DIM) < (HEAD_DIM // 2)

    def proj(c):
        return jnp.dot(xn, w_ref[:, c * A_QK:(c + 1) * A_QK],
                       preferred_element_type=F32)

    def rope(z, out_ref):
        for s in range(A_QK // LANES):
            zs = z[:, s * LANES:(s + 1) * LANES]
            swapped = jnp.where(first_half,
                                pltpu.roll(zs, LANES - HEAD_DIM // 2, 1),
                                pltpu.roll(zs, HEAD_DIM // 2, 1))
            out_ref[:, s * LANES:(s + 1) * LANES] = zs * cos + swapped * sin

    rope(proj(0), q_ref)
    rope(proj(1), k_ref)
    v_ref[...] = proj(2)
    bg_ref[...] = proj(3)
    u_ref[...] = proj(4) * proj(5)


def _even_in(x, nw, w, li, cos, sin, tm):
    n = x.shape[0]
    row = lambda i: (i, 0)
    pos_tiles = cos.shape[0] // tm
    pos = lambda i: (i % pos_tiles, 0)
    out = jax.ShapeDtypeStruct((n, A_QK), F32)
    return pl.pallas_call(
        _even_in_kernel,
        grid=(n // tm,),
        in_specs=[pl.BlockSpec((tm, D_MODEL), row),
                  _const_spec((1, D_MODEL)),
                  _layer_spec(w, li),
                  pl.BlockSpec((tm, LANES), pos),
                  pl.BlockSpec((tm, LANES), pos)],
        out_specs=[pl.BlockSpec((tm, A_QK), row)] * 5,
        out_shape=[out] * 5,
        compiler_params=_cparams(("parallel",)),
        name="even_in",
    )(x, nw, w, cos, sin)


def _flash_kernel(ii_ref, jj_ref, lam_ref, q_ref, k_ref, v_ref, sw_ref, o_ref,
                  qs_ref, m_ref, acc_ref, *, tq, tk, lam_scale):
    pair = pl.program_id(2)
    j = jj_ref[pair]
    d = j - ii_ref[pair] * (tq // tk)
    per_part = tk // FLASH_ROWS

    @pl.when(j == 0)
    def _init():
        qs_ref[...] = (q_ref[...] * (ATTN_SCALE * LOG2E)).astype(BF16)
        m_ref[...] = jnp.full(m_ref.shape, -jnp.inf, F32)
        acc_ref[...] = jnp.zeros(acc_ref.shape, F32)

    def step(diag):
        kf = k_ref[...]
        lane = lax.broadcasted_iota(jnp.int32, kf.shape, 1)
        v = v_ref[...].astype(BF16)
        v_ext = jnp.concatenate([v, jnp.ones_like(v)], axis=1)
        tri = (lax.broadcasted_iota(jnp.int32, (FLASH_ROWS, FLASH_ROWS), 1)
               <= lax.broadcasted_iota(jnp.int32, (FLASH_ROWS, FLASH_ROWS), 0))
        for mp in range(2):
            keep = (lane < HEAD_DIM) if mp == 0 else (lane >= HEAD_DIM)
            k = jnp.where(keep, kf, 0.0).astype(BF16)
            for rc in range(tq // FLASH_ROWS):
                part, local = divmod(rc, per_part)
                if diag is not None and part < diag:
                    continue
                on_diag = diag is not None and part == diag
                rows = slice(rc * FLASH_ROWS, (rc + 1) * FLASH_ROWS)
                first = on_diag and (local + 1) * FLASH_ROWS <= tk // 2
                ncol = tk // 2 if first else tk
                s = lax.dot_general(qs_ref[rows, :], k[0:ncol], (((1,), (1,)), ((), ())),
                                    preferred_element_type=F32)
                if on_diag:
                    lo = local * FLASH_ROWS
                    hi = lo + FLASH_ROWS
                    parts = [s[:, 0:lo]] if local > 0 else []
                    parts.append(jnp.where(tri, s[:, lo:hi], -jnp.inf))
                    if hi < ncol:
                        parts.append(jnp.full((FLASH_ROWS, ncol - hi), -jnp.inf, F32))
                    s = jnp.concatenate(parts, axis=1)
                m_prev = m_ref[mp, rows, :]
                m_new = jnp.maximum(m_prev, jnp.max(s, axis=-1, keepdims=True))
                alpha = jnp.exp2(m_prev - m_new)
                p = jnp.exp2(s - _tile_lanes(m_new, ncol // LANES))
                pv = jnp.dot(p.astype(BF16), v_ext[0:ncol], preferred_element_type=F32)
                acc_ref[mp, rows, :] = _tile_lanes(alpha, 2) * acc_ref[mp, rows, :] + pv
                m_ref[mp, rows, :] = m_new

    @pl.when(d < 0)
    def _visible():
        step(None)

    for diag in range(tq // tk):
        @pl.when(d == diag)
        def _diagonal(diag=diag):
            step(diag)
            if diag == tq // tk - 1:
                a0 = acc_ref[0]
                a1 = acc_ref[1]
                o = (a0[:, 0:LANES] / a0[:, LANES:2 * LANES]
                     - lam_ref[0] * (a1[:, 0:LANES] / a1[:, LANES:2 * LANES]))
                o_ref[...] = _rms(o, sw_ref[...]) * lam_scale


def _flash(lam, q, k, v, sw, nb, t, lam_scale, tq, tk):
    nq, nk, ratio = t // tq, t // tk, tq // tk
    pairs = [(i, j) for i in range(nq) for j in range(ratio * (i + 1))]
    ii = jnp.asarray(np.array([p[0] for p in pairs], np.int32))
    jj = jnp.asarray(np.array([p[1] for p in pairs], np.int32))
    kern = functools.partial(_flash_kernel, tq=tq, tk=tk, lam_scale=lam_scale)
    qmap = lambda b, h, p, ii, jj: (b * nq + ii[p], h)
    kmap = lambda b, h, p, ii, jj: (b * nk + jj[p], h)
    grid_spec = pltpu.PrefetchScalarGridSpec(
        num_scalar_prefetch=2,
        grid=(nb, A_HEADS, len(pairs)),
        in_specs=[pl.BlockSpec(memory_space=pltpu.SMEM),
                  pl.BlockSpec((tq, LANES), qmap),
                  pl.BlockSpec((tk, LANES), kmap),
                  pl.BlockSpec((tk, LANES), kmap),
                  pl.BlockSpec((1, LANES), lambda b, h, p, ii, jj: (0, 0))],
        out_specs=pl.BlockSpec((tq, LANES), qmap),
        scratch_shapes=[pltpu.VMEM((tq, LANES), BF16),
                        pltpu.VMEM((2, tq, LANES), F32),
                        pltpu.VMEM((2, tq, 2 * LANES), F32)])
    return pl.pallas_call(
        kern,
        grid_spec=grid_spec,
        out_shape=jax.ShapeDtypeStruct((nb * t, A_V), F32),
        compiler_params=_cparams(("parallel", "parallel", "arbitrary")),
        name="flash_diff_attn",
    )(ii, jj, lam, q, k, v, sw)


DEC_ROWS = 16


def _decode_attn_kernel(pt_ref, lam_ref, q_ref, kc_ref, vc_ref, sw_ref, *refs,
                        pages, lam_scale):
    del pt_ref
    kp = refs[0:pages]
    vp = refs[pages:2 * pages]
    o_ref = refs[2 * pages]
    qbd_ref, m_ref, l_ref, acc_ref = refs[2 * pages + 1:]
    j = pl.program_id(1)

    @pl.when(j == 0)
    def _init():
        q = jnp.broadcast_to(q_ref[0] * (ATTN_SCALE * LOG2E), (DEC_ROWS, A_QK))
        row = lax.broadcasted_iota(jnp.int32, q.shape, 0)
        blk = lax.broadcasted_iota(jnp.int32, q.shape, 1) // HEAD_DIM
        qbd_ref[...] = jnp.where(row == blk, q, 0.0)
        m_ref[...] = jnp.full(m_ref.shape, -jnp.inf, F32)
        l_ref[...] = jnp.zeros(l_ref.shape, F32)
        acc_ref[...] = jnp.zeros(acc_ref.shape, F32)

    qbd = qbd_ref[...].astype(BF16)
    s = jnp.concatenate(
        [jnp.dot(qbd, kp[p][...].astype(BF16), preferred_element_type=F32)
         for p in range(pages)], axis=1)
    m_prev = m_ref[...]
    m_new = jnp.maximum(m_prev, jnp.max(s, axis=-1, keepdims=True))
    alpha = jnp.exp2(m_prev - m_new)
    p_all = jnp.exp2(s - _tile_lanes(m_new, pages))
    l_ref[...] = alpha * l_ref[...] + jnp.sum(p_all, axis=-1, keepdims=True)
    p_bf = p_all.astype(BF16)
    for h in range(A_HEADS):
        pv = jnp.zeros((DEC_ROWS, A_VDIM), F32)
        for p in range(pages):
            v_h = vp[p][pl.ds(h, PAGE_SIZE, stride=A_HEADS), :]
            pv = pv + jnp.dot(p_bf[:, p * PAGE_SIZE:(p + 1) * PAGE_SIZE],
                              v_h.astype(BF16), preferred_element_type=F32)
        acc_ref[h] = alpha * acc_ref[h] + pv
    m_ref[...] = m_new

    @pl.when(j == pl.num_programs(1) - 1)
    def _finish():
        s_cur = jnp.sum(qbd_ref[...] * kc_ref[0], axis=-1, keepdims=True)
        m_old = m_ref[...]
        m_fin = jnp.maximum(m_old, s_cur)
        a_fin = jnp.exp2(m_old - m_fin)
        p_cur = jnp.exp2(s_cur - m_fin)
        l_fin = a_fin * l_ref[...] + p_cur
        lam = lam_ref[0]
        sw = sw_ref[...]
        vc = vc_ref[0]
        for h in range(A_HEADS):
            cols = slice(h * A_VDIM, (h + 1) * A_VDIM)
            a = (a_fin * acc_ref[h] + p_cur * vc[:, cols]) / l_fin
            o = a[2 * h:2 * h + 1, :] - lam * a[2 * h + 1:2 * h + 2, :]
            o_ref[0, :, cols] = _rms(o, sw) * lam_scale


def _decode_attn(page_table, lam, qd, kd, vd, sw, cache_kt, cache_vf, e, lam_scale,
                 pages):
    nb, n_pages = page_table.shape
    kern = functools.partial(_decode_attn_kernel, pages=pages, lam_scale=lam_scale)
    tok = lambda b, j, pt: (b, 0, 0)

    def page_spec(p):
        return pl.BlockSpec((None, None, A_QK, PAGE_SIZE),
                            lambda b, j, pt: (e, pt[b, j * pages + p], 0, 0))

    grid_spec = pltpu.PrefetchScalarGridSpec(
        num_scalar_prefetch=1,
        grid=(nb, n_pages // pages),
        in_specs=([pl.BlockSpec(memory_space=pltpu.SMEM),
                   pl.BlockSpec((1, 1, A_QK), tok),
                   pl.BlockSpec((1, 1, A_QK), tok),
                   pl.BlockSpec((1, 1, A_V), tok),
                   pl.BlockSpec((1, LANES), lambda b, j, pt: (0, 0))]
                  + [page_spec(p) for p in range(pages)] * 2),
        out_specs=pl.BlockSpec((1, 1, A_V), tok),
        scratch_shapes=[pltpu.VMEM((DEC_ROWS, A_QK), F32),
                        pltpu.VMEM((DEC_ROWS, LANES), F32),
                        pltpu.VMEM((DEC_ROWS, LANES), F32),
                        pltpu.VMEM((A_HEADS, DEC_ROWS, A_VDIM), F32)])
    return pl.pallas_call(
        kern,
        grid_spec=grid_spec,
        out_shape=jax.ShapeDtypeStruct((nb, 1, A_V), F32),
        compiler_params=_cparams(("parallel", "arbitrary")),
        name="decode_diff_attn",
    )(page_table, lam, qd, kd, vd, sw, *([cache_kt] * pages), *([cache_vf] * pages))


def _ffn_rows(x, nw_in_ref, wi_ref, wo_ref, nw_out_ref):
    xn = _rms(x, nw_in_ref[...]).astype(BF16)
    y = jnp.zeros(x.shape, F32)
    for c in range(D_FF // FFN_CHUNK):
        lo = c * FFN_CHUNK
        gate = jnp.dot(xn, wi_ref[:, lo:lo + FFN_CHUNK], preferred_element_type=F32)
        up = jnp.dot(xn, wi_ref[:, D_FF + lo:D_FF + lo + FFN_CHUNK],
                     preferred_element_type=F32)
        act = (gate * _sigmoid(gate) * up).astype(BF16)
        y = y + jnp.dot(act, wo_ref[lo:lo + FFN_CHUNK, :], preferred_element_type=F32)
    return x + _rms(y, nw_out_ref[...])


def _even_tail_kernel(*refs, tm, seq_tiles, explicit_prev):
    if explicit_prev:
        o_ref, bg_ref, u_ref, um1_ref, um2_ref = refs[:5]
        rest = refs[5:]
        u = u_ref[...]
        um1 = um1_ref[...]
        um2 = um2_ref[...]
    else:
        o_ref, bg_ref, u_ref, halo_ref = refs[:4]
        rest = refs[4:]
        u = u_ref[...]
        halo = halo_ref[...]
        halo = jnp.where(pl.program_id(0) % seq_tiles == 0, 0.0, halo)
        row = lax.broadcasted_iota(jnp.int32, u.shape, 0)
        h1 = halo[SUBLANES - 1:SUBLANES, :]
        h2 = halo[SUBLANES - 2:SUBLANES - 1, :]
        um1 = jnp.where(row == 0, h1, pltpu.roll(u, 1, 0))
        um2 = jnp.where(row == 0, h2, jnp.where(row == 1, h1, pltpu.roll(u, 2, 0)))
    cw_ref, w_ref, x_ref, nw_ref, nw_in_ref, wi_ref, wo_ref, nw_out_ref, out_ref = rest
    cw = cw_ref[...]
    conv = um2 * cw[0:1, :] + um1 * cw[1:2, :] + u * cw[2:3, :]
    yb = bg_ref[...] * conv
    y = (jnp.dot(o_ref[...].astype(BF16), w_ref[0:A_V, :], preferred_element_type=F32)
         + jnp.dot(yb.astype(BF16), w_ref[A_V:A_V + B_WIDTH, :],
                   preferred_element_type=F32))
    x_mid = x_ref[...] + _rms(y, nw_ref[...])
    out_ref[...] = _ffn_rows(x_mid, nw_in_ref, wi_ref, wo_ref, nw_out_ref)


def _even_tail(o, bg, u, prev, cw, w, li, x, nw, wi, wo, layer, tm, seq_len):
    n = x.shape[0]
    row = lambda i: (i, 0)
    half = pl.BlockSpec((tm, B_WIDTH), row)
    vec = _const_spec((1, D_MODEL))
    explicit_prev = isinstance(prev, tuple)
    if explicit_prev:
        prev_args = list(prev)
        prev_specs = [half, half]
    else:
        prev_args = [u]
        prev_specs = [pl.BlockSpec(
            (SUBLANES, B_WIDTH),
            lambda i: (jnp.maximum(i * (tm // SUBLANES) - 1, 0), 0))]
    kern = functools.partial(_even_tail_kernel, tm=tm, seq_tiles=seq_len // tm,
                             explicit_prev=explicit_prev)
    return pl.pallas_call(
        kern,
        grid=(n // tm,),
        in_specs=([half, half, half] + prev_specs
                  + [_const_spec((CONV_W, B_WIDTH)),
                     _layer_spec(w, li),
                     pl.BlockSpec((tm, D_MODEL), row),
                     vec, vec, _layer_spec(wi, layer), _layer_spec(wo, layer), vec]),
        out_specs=pl.BlockSpec((tm, D_MODEL), row),
        out_shape=jax.ShapeDtypeStruct((n, D_MODEL), F32),
        compiler_params=_cparams(("parallel",)),
        name="even_tail",
    )(o, bg, u, *prev_args, cw, w, x, nw[1], nw[2], wi, wo, nw[3])


def _odd_in_kernel(x_ref, nw_ref, w_ref, lb_ref, z_ref):
    xn = _rms(x_ref[...], nw_ref[...]).astype(BF16)

    def proj(c):
        return jnp.dot(xn, w_ref[:, c * D_MODEL:(c + 1) * D_MODEL],
                       preferred_element_type=F32)

    zq = proj(0)
    z_ref[:, 0:HG_FDIM] = zq * _sigmoid(zq)
    lb = lb_ref[...]
    z_ref[:, HG_FDIM:2 * HG_FDIM] = lb + (1.0 - lb) * _sigmoid(proj(1))
    z_ref[:, 2 * HG_FDIM:2 * HG_FDIM + D_MODEL] = proj(2)
    zg = proj(3)
    z_ref[:, 2 * HG_FDIM + D_MODEL:ODD_IN] = zg * _sigmoid(zg)


def _odd_in(x, nw, w, li, lb, tm):
    n = x.shape[0]
    row = lambda i: (i, 0)
    return pl.pallas_call(
        _odd_in_kernel,
        grid=(n // tm,),
        in_specs=[pl.BlockSpec((tm, D_MODEL), row),
                  _const_spec((1, D_MODEL)),
                  _layer_spec(w, li),
                  _const_spec((1, HG_FDIM))],
        out_specs=pl.BlockSpec((tm, ODD_IN), row),
        out_shape=jax.ShapeDtypeStruct((n, ODD_IN), F32),
        compiler_params=_cparams(("parallel",)),
        name="odd_in",
    )(x, nw, w, lb)


def _gla_tables():
    c = GLA_CHUNK
    t = np.arange(c)[:, None]
    r = np.arange(c)[None, :]
    mats = []
    for lev in range(GLA_SUM_LEVELS):
        half = 1 << lev
        mid = (t // (2 * half)) * (2 * half) + half
        mats.append(np.where(t >= mid, (r >= mid) & (r <= t), (r > t) & (r < mid)))
    mats.append(r <= t)
    sums = np.concatenate(mats, axis=0).astype(np.float32)
    high_bit = np.floor(np.log2(np.maximum(t ^ r, 1))).astype(np.int32)
    level = np.where(t > r, high_bit, np.where(t == r, GLA_LEVELS, GLA_LEVELS + 1))
    sign = np.stack([np.where((t >> lev) & 1 == 1, 1.0, -1.0) * np.ones((1, HG_DK))
                     for lev in range(GLA_SUM_LEVELS, GLA_LEVELS)]).astype(np.float32)
    return (jnp.asarray(sums, BF16), jnp.asarray(level.astype(np.int32)),
            jnp.asarray(sign))


def _gla_kernel(q_ref, f_ref, v_ref, g_ref, gw_ref, sums_ref, level_ref, sign_ref,
                og_ref, sout_ref, st_ref, o_s, *, rows):
    c = GLA_CHUNK
    tb = pl.program_id(2)

    @pl.when(tb == 0)
    def _zero_state():
        st_ref[...] = jnp.zeros(st_ref.shape, F32)

    q_all = q_ref[...]
    fg = f_ref[...]
    logf_all = jnp.log(fg) * LOG2E
    k_all = 1.0 - fg
    v_all = v_ref[...]
    level = level_ref[...]
    row_id = lax.broadcasted_iota(jnp.int32, (c, HG_DK), 0)
    contract_last = (((1,), (1,)), ((), ()))

    st = st_ref[...]
    for ch in range(rows // c):
        sl = slice(ch * c, (ch + 1) * c)
        q = q_all[sl]
        k = k_all[sl]
        v = v_all[sl].astype(BF16)
        logf = logf_all[sl]
        hi = logf.astype(BF16)
        lo = (logf - hi.astype(F32)).astype(BF16)
        w = jnp.dot(sums_ref[...], jnp.concatenate([hi, lo], axis=1),
                    preferred_element_type=F32)
        w = w[:, 0:HG_DK] + w[:, HG_DK:2 * HG_DK]
        cum = w[GLA_SUM_LEVELS * c:(GLA_SUM_LEVELS + 1) * c]
        a = jnp.zeros((c, c), F32)
        for lev in range(GLA_LEVELS):
            if lev < GLA_SUM_LEVELS:
                expo = w[lev * c:(lev + 1) * c]
            else:
                half = 1 << lev
                at_split = jnp.concatenate(
                    [jnp.broadcast_to(cum[b0 + half - 1:b0 + half, :], (2 * half, HG_DK))
                     for b0 in range(0, c, 2 * half)], axis=0)
                expo = (cum - at_split) * sign_ref[lev - GLA_SUM_LEVELS]
            upper = ((row_id >> lev) & 1) == 1
            y = (jnp.where(upper, q, k) * jnp.exp2(expo)).astype(BF16)
            a = jnp.where(level == lev,
                          lax.dot_general(y, y, contract_last,
                                          preferred_element_type=F32), a)
        a = jnp.where(level == GLA_LEVELS,
                      lax.dot_general(q.astype(BF16), k.astype(BF16), contract_last,
                                      preferred_element_type=F32), a)
        o_intra = jnp.dot(a.astype(BF16), v, preferred_element_type=F32)
        total = cum[c - 1:c, :]
        qh = (q * jnp.exp2(cum)).astype(BF16)
        kh = (k * jnp.exp2(total - cum)).astype(BF16)
        o_inter = lax.dot_general(qh, st.astype(BF16), contract_last,
                                  preferred_element_type=F32)
        st = st * jnp.exp2(total) + lax.dot_general(v, kh, (((0,), (0,)), ((), ())),
                                                    preferred_element_type=F32)
        o_s[sl, :] = o_inter + o_intra
    st_ref[...] = st
    og_ref[...] = _rms(o_s[...], gw_ref[...]) * g_ref[...]

    @pl.when(tb == pl.num_programs(2) - 1)
    def _store_state():
        sout_ref[0, 0] = st.T


def _gla(z, gw, nseq, t, rows):
    nt = t // rows
    sums, level, sign = _gla_tables()

    def col_spec(base):
        return pl.BlockSpec((rows, HG_DK),
                            lambda s, h, tb: (s * nt + tb, base + h))

    kern = functools.partial(_gla_kernel, rows=rows)
    return pl.pallas_call(
        kern,
        grid=(nseq, HG_HEADS, nt),
        in_specs=[col_spec(0), col_spec(HG_HEADS), col_spec(2 * HG_HEADS),
                  col_spec(3 * HG_HEADS),
                  pl.BlockSpec((1, HG_DV), lambda s, h, tb: (0, 0)),
                  _const_spec(sums.shape), _const_spec(level.shape),
                  _const_spec(sign.shape)],
        out_specs=[pl.BlockSpec((rows, HG_DV), lambda s, h, tb: (s * nt + tb, h)),
                   pl.BlockSpec((1, 1, HG_DK, HG_DV), lambda s, h, tb: (s, h, 0, 0))],
        out_shape=[jax.ShapeDtypeStruct((nseq * t, D_MODEL), F32),
                   jax.ShapeDtypeStruct((nseq, HG_HEADS, HG_DK, HG_DV), F32)],
        scratch_shapes=[pltpu.VMEM((HG_DV, HG_DK), F32),
                        pltpu.VMEM((rows, HG_DV), F32)],
        compiler_params=_cparams(("parallel", "parallel", "arbitrary")),
        name="hgrn2_gla",
    )(z, z, z, z, gw, sums, level, sign)


def _gla_decode_kernel(z_ref, gw_ref, s0_ref, og_ref, sout_ref):
    z = z_ref[0]
    eye = (lax.broadcasted_iota(jnp.int32, (HG_DK, HG_DK), 0)
           == lax.broadcasted_iota(jnp.int32, (HG_DK, HG_DK), 1))

    def column(row_vec):
        return jnp.sum(jnp.where(eye, row_vec, 0.0), axis=-1, keepdims=True)

    gw = gw_ref[...]
    for h in range(HG_HEADS):
        cols = slice(h * HG_DK, (h + 1) * HG_DK)
        q = z[:, cols]
        fg = z[:, HG_FDIM + h * HG_DK:HG_FDIM + (h + 1) * HG_DK]
        v = z[:, 2 * HG_FDIM + h * HG_DV:2 * HG_FDIM + (h + 1) * HG_DV]
        gate = z[:, 2 * HG_FDIM + D_MODEL + h * HG_DV:2 * HG_FDIM + D_MODEL + (h + 1) * HG_DV]
        s_new = column(fg) * s0_ref[h] + column(1.0 - fg) * v
        sout_ref[0, h] = s_new
        o = jnp.sum(column(q) * s_new, axis=0, keepdims=True)
        og_ref[0, :, cols] = _rms(o, gw) * gate


def _gla_decode(z, gw, states, li):
    nseq = z.shape[0]
    tok = lambda s: (s, 0, 0)
    heads = (HG_HEADS, HG_DK, HG_DV)
    return pl.pallas_call(
        _gla_decode_kernel,
        grid=(nseq,),
        in_specs=[pl.BlockSpec((1, 1, ODD_IN), tok),
                  _const_spec((1, HG_DV)),
                  pl.BlockSpec((None, None) + heads, lambda s: (li, s, 0, 0, 0))],
        out_specs=[pl.BlockSpec((1, 1, D_MODEL), tok),
                   pl.BlockSpec((1,) + heads, lambda s: (s, 0, 0, 0))],
        out_shape=[jax.ShapeDtypeStruct((nseq, 1, D_MODEL), F32),
                   jax.ShapeDtypeStruct((nseq,) + heads, F32)],
        compiler_params=_cparams(("parallel",)),
        name="hgrn2_decode",
    )(z.reshape(nseq, 1, ODD_IN), gw, states)


def _odd_tail_kernel(a_ref, w_ref, x_ref, nw_ref, nw_in_ref, wi_ref, wo_ref, nw_out_ref,
                     out_ref):
    y = jnp.dot(a_ref[...].astype(BF16), w_ref[...], preferred_element_type=F32)
    x_mid = x_ref[...] + _rms(y, nw_ref[...])
    out_ref[...] = _ffn_rows(x_mid, nw_in_ref, wi_ref, wo_ref, nw_out_ref)


def _odd_tail(a, w, li, x, nw, wi, wo, layer, tm):
    n = x.shape[0]
    row = lambda i: (i, 0)
    vec = _const_spec((1, D_MODEL))
    return pl.pallas_call(
        _odd_tail_kernel,
        grid=(n // tm,),
        in_specs=[pl.BlockSpec((tm, D_MODEL), row),
                  _layer_spec(w, li),
                  pl.BlockSpec((tm, D_MODEL), row),
                  vec, vec, _layer_spec(wi, layer), _layer_spec(wo, layer), vec],
        out_specs=pl.BlockSpec((tm, D_MODEL), row),
        out_shape=jax.ShapeDtypeStruct((n, D_MODEL), F32),
        compiler_params=_cparams(("parallel",)),
        name="odd_tail",
    )(a, w, x, nw[1], nw[2], wi, wo, nw[3])


def _rope_tables(pos):
    half = HEAD_DIM // 2
    inv = ROPE_THETA ** (-jnp.arange(half, dtype=F32) / half)
    ang = pos.astype(F32)[:, None] * inv[None, :]
    cos = jnp.tile(jnp.cos(ang), (1, LANES // half))
    sin = jnp.tile(jnp.concatenate([-jnp.sin(ang), jnp.sin(ang)], axis=1),
                   (1, LANES // HEAD_DIM))
    return cos, sin


def kernel(x_prompt, x_sample, cache_k, cache_v, state_conv, state_hgrn, page_table,
           norm_w, w_in_even, w_out_even, conv_w, lambda_q1, lambda_k1, lambda_q2,
           lambda_k2, subln_w, w_in_odd, w_out_odd, hgrn_lb, gnorm_w, w_ffn_in,
           w_ffn_out):
    nb, t, _ = x_prompt.shape
    nd, td, _ = x_sample.shape
    assert td == 1
    depth = norm_w.shape[0]
    n_pages = page_table.shape[1]
    past_len = n_pages * PAGE_SIZE
    tm_p = 512
    tm_d = nd
    dec_pages = 32
    gla_rows = 4096
    flash_tq, flash_tk = 2048, 1024

    cos_p, sin_p = _rope_tables(jnp.arange(t))
    cos_d, sin_d = _rope_tables(jnp.full((nd,), past_len))
    p_lb = jax.nn.softmax(hgrn_lb.astype(F32), axis=0)
    lower_bounds = jnp.cumsum(p_lb, axis=0) - p_lb[0]

    n_even, n_phys = cache_k.shape[:2]
    ck = jnp.transpose(cache_k, (0, 1, 3, 4, 5, 2)).reshape(n_even, n_phys, A_QK, PAGE_SIZE)
    cv = cache_v.reshape(n_even, n_phys, PAGE_SIZE * A_HEADS, A_VDIM)
    nw = norm_w.reshape(depth, 4, 1, D_MODEL)
    w_in_e, w_out_e = w_in_even.astype(BF16), w_out_even.astype(BF16)
    w_in_o, w_out_o = w_in_odd.astype(BF16), w_out_odd.astype(BF16)
    w_ffn_i, w_ffn_o = w_ffn_in.astype(BF16), w_ffn_out.astype(BF16)

    xp = x_prompt.reshape(nb * t, D_MODEL)
    xs = x_sample.reshape(nd, D_MODEL)
    kp_l, vp_l, cp_l, hp_l = [], [], [], []
    ks_l, vs_l, cs_l, hs_l = [], [], [], []
    for layer in range(depth):
        if layer % 2 == 0:
            e = layer // 2
            lam_init = 0.8 - 0.6 * math.exp(-0.3 * layer)
            lam = (jnp.exp(jnp.sum(lambda_q1[e].astype(F32) * lambda_k1[e].astype(F32)))
                   - jnp.exp(jnp.sum(lambda_q2[e].astype(F32) * lambda_k2[e].astype(F32)))
                   + lam_init).reshape(1)
            sw = subln_w[e].reshape(1, A_VDIM)
            cw = conv_w[e]

            q, k, v, bg, u = _even_in(xp, nw[layer, 0], w_in_e, e, cos_p, sin_p, tm_p)
            o = _flash(lam, q, k, v, sw, nb, t, 1.0 - lam_init, flash_tq, flash_tk)
            xp = _even_tail(o, bg, u, None, cw, w_out_e, e, xp, nw[layer], w_ffn_i, w_ffn_o,
                            layer, tm_p, t)

            qd, kd, vd, bgd, ud = _even_in(xs, nw[layer, 0], w_in_e, e, cos_d, sin_d, tm_d)
            od = _decode_attn(page_table, lam, qd.reshape(nd, 1, A_QK),
                              kd.reshape(nd, 1, A_QK), vd.reshape(nd, 1, A_V), sw,
                              ck, cv, e, 1.0 - lam_init, dec_pages)
            sc = state_conv[e]
            xs = _even_tail(od.reshape(nd, A_V), bgd, ud, (sc[:, 1], sc[:, 0]), cw,
                            w_out_e, e, xs, nw[layer], w_ffn_i, w_ffn_o, layer, tm_d, td)

            kp_l.append(k.reshape(nb, t, A_HEADS, 2, HEAD_DIM))
            vp_l.append(v.reshape(nb, t, A_HEADS, A_VDIM))
            cp_l.append(u.reshape(nb, t, B_WIDTH)[:, t - (CONV_W - 1):])
            ks_l.append(kd.reshape(nd, td, A_HEADS, 2, HEAD_DIM))
            vs_l.append(vd.reshape(nd, td, A_HEADS, A_VDIM))
            cs_l.append(jnp.concatenate([sc[:, 1:], ud[:, None, :]], axis=1))
        else:
            o_idx = layer // 2
            lb = lower_bounds[layer].reshape(1, HG_FDIM)
            gw = gnorm_w[o_idx].reshape(1, HG_DV)

            z = _odd_in(xp, nw[layer, 0], w_in_o, o_idx, lb, tm_p)
            og, sp = _gla(z, gw, nb, t, gla_rows)
            xp = _odd_tail(og, w_out_o, o_idx, xp, nw[layer], w_ffn_i, w_ffn_o, layer, tm_p)

            zd = _odd_in(xs, nw[layer, 0], w_in_o, o_idx, lb, tm_d)
            ogd, ss = _gla_decode(zd, gw, state_hgrn, o_idx)
            xs = _odd_tail(ogd.reshape(nd, D_MODEL), w_out_o, o_idx, xs, nw[layer], w_ffn_i,
                           w_ffn_o, layer, tm_d)
            hp_l.append(sp)
            hs_l.append(ss)

    return (xp.reshape(nb, t, D_MODEL), xs.reshape(nd, td, D_MODEL),
            jnp.stack(kp_l), jnp.stack(vp_l), jnp.stack(cp_l), jnp.stack(hp_l),
            jnp.stack(ks_l), jnp.stack(vs_l), jnp.stack(cs_l), jnp.stack(hs_l))
```

```python
import functools
import math

import numpy as np
import jax
import jax.numpy as jnp
from jax import lax
from jax.experimental import pallas as pl
from jax.experimental.pallas import tpu as pltpu

D_MODEL = 1024
PAGE_SIZE = 128
HEAD_DIM = 64
A_HEADS = 4
A_QK = A_HEADS * 2 * HEAD_DIM
A_VDIM = 2 * HEAD_DIM
A_V = A_HEADS * A_VDIM
ATTN_SCALE = HEAD_DIM ** -0.5
ROPE_THETA = 10000.0
B_WIDTH = D_MODEL // 2
CONV_W = 3
EVEN_IN = 2 * A_QK + A_V + 3 * B_WIDTH
HG_HEADS = 8
HG_DK = 128
HG_FDIM = HG_HEADS * HG_DK
HG_DV = D_MODEL // HG_HEADS
ODD_IN = 2 * HG_FDIM + 2 * D_MODEL
D_FF = -(-8 * D_MODEL // (3 * 256)) * 256
EPS = 1e-6
LOG2E = math.log2(math.e)

LANES = 128
SUBLANES = 8
VMEM_LIMIT_BYTES = 56 * 1024 * 1024
GLA_CHUNK = 128
GLA_LEVELS = 7
GLA_SUM_LEVELS = 3
FFN_CHUNK = 256
FLASH_ROWS = 256

F32 = jnp.float32
BF16 = jnp.bfloat16


def _cparams(semantics):
    return pltpu.CompilerParams(dimension_semantics=semantics,
                                vmem_limit_bytes=VMEM_LIMIT_BYTES)


def _rms(x, w):
    return x * lax.rsqrt(jnp.mean(x * x, axis=-1, keepdims=True) + EPS) * w


def _sigmoid(x):
    return 1.0 / (1.0 + jnp.exp(-x))


def _tile_lanes(x, n):
    return x if n == 1 else jnp.concatenate([x] * n, axis=1)


def _layer_spec(stacked, li):
    shape = stacked.shape[1:]
    nd = len(shape)
    return pl.BlockSpec((None,) + shape, lambda *_: (li,) + (0,) * nd,
                        pipeline_mode=pl.Buffered(1))


def _const_spec(shape):
    nd = len(shape)
    return pl.BlockSpec(shape, lambda *_: (0,) * nd, pipeline_mode=pl.Buffered(1))


def _even_in_kernel(x_ref, nw_ref, w_ref, cos_ref, sin_ref,
                    q_ref, k_ref, v_ref, bg_ref, u_ref, vrows_ref):
    xn = _rms(x_ref[...], nw_ref[...]).astype(BF16)
    cos = cos_ref[...]
    sin = sin_ref[...]
    lane = lax.broadcasted_iota(jnp.int32, cos.shape, 1)
    first_half = (lane % HEAD_DIM) < (HEAD_DIM // 2)

    def proj(c):
        return jnp.dot(xn, w_ref[:, c * A_QK:(c + 1) * A_QK],
                       preferred_element_type=F32)

    def rope(z, out_ref):
        for s in range(A_QK // LANES):
            zs = z[:, s * LANES:(s + 1) * LANES]
            swapped = jnp.where(first_half,
                                pltpu.roll(zs, LANES - HEAD_DIM // 2, 1),
                                pltpu.roll(zs, HEAD_DIM // 2, 1))
            out_ref[:, s * LANES:(s + 1) * LANES] = zs * cos + swapped * sin

    rope(proj(0), q_ref)
    rope(proj(1), k_ref)
    v = proj(2)
    v_ref[...] = v
    for h in range(A_HEADS):
        vrows_ref[pl.ds(h, v.shape[0], stride=A_HEADS), :] = v[:, h * A_VDIM:(h + 1) * A_VDIM]
    bg_ref[...] = proj(3)
    u_ref[...] = proj(4) * proj(5)


def _even_in(x, nw, w, li, cos, sin, tm):
    n = x.shape[0]
    row = lambda i: (i, 0)
    pos_tiles = cos.shape[0] // tm
    pos = lambda i: (i % pos_tiles, 0)
    out = jax.ShapeDtypeStruct((n, A_QK), F32)
    return pl.pallas_call(
        _even_in_kernel,
        grid=(n // tm,),
        in_specs=[pl.BlockSpec((tm, D_MODEL), row),
                  _const_spec((1, D_MODEL)),
                  _layer_spec(w, li),
                  pl.BlockSpec((tm, LANES), pos),
                  pl.BlockSpec((tm, LANES), pos)],
        out_specs=[pl.BlockSpec((tm, A_QK), row)] * 5
        + [pl.BlockSpec((tm * A_HEADS, A_VDIM), row)],
        out_shape=[out] * 5 + [jax.ShapeDtypeStruct((n * A_HEADS, A_VDIM), F32)],
        compiler_params=_cparams(("parallel",)),
        name="even_in",
    )(x, nw, w, cos, sin)


def _flash_kernel(ii_ref, jj_ref, lam_ref, q_ref, k_ref, v_ref, sw_ref, o_ref,
                  qs_ref, m_ref, acc_ref, *, tq, tk, lam_scale):
    pair = pl.program_id(2)
    j = jj_ref[pair]
    d = j - ii_ref[pair] * (tq // tk)
    per_part = tk // FLASH_ROWS

    @pl.when(j == 0)
    def _init():
        qs_ref[...] = (q_ref[...] * (ATTN_SCALE * LOG2E)).astype(BF16)
        m_ref[...] = jnp.full(m_ref.shape, -jnp.inf, F32)
        acc_ref[...] = jnp.zeros(acc_ref.shape, F32)

    def step(diag):
        kf = k_ref[...]
        lane = lax.broadcasted_iota(jnp.int32, kf.shape, 1)
        v = v_ref[...].astype(BF16)
        v_ext = jnp.concatenate([v, jnp.ones_like(v)], axis=1)
        tri = (lax.broadcasted_iota(jnp.int32, (FLASH_ROWS, FLASH_ROWS), 1)
               <= lax.broadcasted_iota(jnp.int32, (FLASH_ROWS, FLASH_ROWS), 0))
        for mp in range(2):
            keep = (lane < HEAD_DIM) if mp == 0 else (lane >= HEAD_DIM)
            k = jnp.where(keep, kf, 0.0).astype(BF16)
            for rc in range(tq // FLASH_ROWS):
                part, local = divmod(rc, per_part)
                if diag is not None and part < diag:
                    continue
                on_diag = diag is not None and part == diag
                rows = slice(rc * FLASH_ROWS, (rc + 1) * FLASH_ROWS)
                first = on_diag and (local + 1) * FLASH_ROWS <= tk // 2
                ncol = tk // 2 if first else tk
                s = lax.dot_general(qs_ref[rows, :], k[0:ncol], (((1,), (1,)), ((), ())),
                                    preferred_element_type=F32)
                if on_diag:
                    lo = local * FLASH_ROWS
                    hi = lo + FLASH_ROWS
                    parts = [s[:, 0:lo]] if local > 0 else []
                    parts.append(jnp.where(tri, s[:, lo:hi], -jnp.inf))
                    if hi < ncol:
                        parts.append(jnp.full((FLASH_ROWS, ncol - hi), -jnp.inf, F32))
                    s = jnp.concatenate(parts, axis=1)
                m_prev = m_ref[mp, rows, :]
                m_new = jnp.maximum(m_prev, jnp.max(s, axis=-1, keepdims=True))
                alpha = jnp.exp2(m_prev - m_new)
                p = jnp.exp2(s - _tile_lanes(m_new, ncol // LANES))
                pv = jnp.dot(p.astype(BF16), v_ext[0:ncol], preferred_element_type=F32)
                acc_ref[mp, rows, :] = _tile_lanes(alpha, 2) * acc_ref[mp, rows, :] + pv
                m_ref[mp, rows, :] = m_new

    @pl.when(d < 0)
    def _visible():
        step(None)

    for diag in range(tq // tk):
        @pl.when(d == diag)
        def _diagonal(diag=diag):
            step(diag)
            if diag == tq // tk - 1:
                a0 = acc_ref[0]
                a1 = acc_ref[1]
                o = (a0[:, 0:LANES] / a0[:, LANES:2 * LANES]
                     - lam_ref[0] * (a1[:, 0:LANES] / a1[:, LANES:2 * LANES]))
                o_ref[...] = _rms(o, sw_ref[...]) * lam_scale


def _flash(lam, q, k, v, sw, nb, t, lam_scale, tq, tk):
    nq, nk, ratio = t // tq, t // tk, tq // tk
    pairs = [(i, j) for i in range(nq) for j in range(ratio * (i + 1))]
    ii = jnp.asarray(np.array([p[0] for p in pairs], np.int32))
    jj = jnp.asarray(np.array([p[1] for p in pairs], np.int32))
    kern = functools.partial(_flash_kernel, tq=tq, tk=tk, lam_scale=lam_scale)
    qmap = lambda b, h, p, ii, jj: (b * nq + ii[p], h)
    kmap = lambda b, h, p, ii, jj: (b * nk + jj[p], h)
    grid_spec = pltpu.PrefetchScalarGridSpec(
        num_scalar_prefetch=2,
        grid=(nb, A_HEADS, len(pairs)),
        in_specs=[pl.BlockSpec(memory_space=pltpu.SMEM),
                  pl.BlockSpec((tq, LANES), qmap),
                  pl.BlockSpec((tk, LANES), kmap),
                  pl.BlockSpec((tk, LANES), kmap),
                  pl.BlockSpec((1, LANES), lambda b, h, p, ii, jj: (0, 0))],
        out_specs=pl.BlockSpec((tq, LANES), qmap),
        scratch_shapes=[pltpu.VMEM((tq, LANES), BF16),
                        pltpu.VMEM((2, tq, LANES), F32),
                        pltpu.VMEM((2, tq, 2 * LANES), F32)])
    return pl.pallas_call(
        kern,
        grid_spec=grid_spec,
        out_shape=jax.ShapeDtypeStruct((nb * t, A_V), F32),
        compiler_params=_cparams(("parallel", "parallel", "arbitrary")),
        name="flash_diff_attn",
    )(ii, jj, lam, q, k, v, sw)


DEC_ROWS = 16


def _decode_attn_kernel(pt_ref, lam_ref, q_ref, kc_ref, vc_ref, sw_ref, *refs,
                        pages, lam_scale):
    del pt_ref
    kp = refs[0:pages]
    vp = refs[pages:2 * pages]
    o_ref = refs[2 * pages]
    qbd_ref, m_ref, l_ref, acc_ref = refs[2 * pages + 1:]
    j = pl.program_id(1)

    @pl.when(j == 0)
    def _init():
        q = jnp.broadcast_to(q_ref[0] * (ATTN_SCALE * LOG2E), (DEC_ROWS, A_QK))
        row = lax.broadcasted_iota(jnp.int32, q.shape, 0)
        blk = lax.broadcasted_iota(jnp.int32, q.shape, 1) // HEAD_DIM
        qbd_ref[...] = jnp.where(row == blk, q, 0.0)
        m_ref[...] = jnp.full(m_ref.shape, -jnp.inf, F32)
        l_ref[...] = jnp.zeros(l_ref.shape, F32)
        acc_ref[...] = jnp.zeros(acc_ref.shape, F32)

    qbd = qbd_ref[...].astype(BF16)
    s = jnp.concatenate(
        [jnp.dot(qbd, kp[p][...].astype(BF16), preferred_element_type=F32)
         for p in range(pages)], axis=1)
    m_prev = m_ref[...]
    m_new = jnp.maximum(m_prev, jnp.max(s, axis=-1, keepdims=True))
    alpha = jnp.exp2(m_prev - m_new)
    p_all = jnp.exp2(s - _tile_lanes(m_new, pages))
    l_ref[...] = alpha * l_ref[...] + jnp.sum(p_all, axis=-1, keepdims=True)
    p_bf = p_all.astype(BF16)
    for h in range(A_HEADS):
        pv = jnp.zeros((DEC_ROWS, A_VDIM), F32)
        for p in range(pages):
            v_h = vp[p][pl.ds(h, PAGE_SIZE, stride=A_HEADS), :]
            pv = pv + jnp.dot(p_bf[:, p * PAGE_SIZE:(p + 1) * PAGE_SIZE],
                              v_h.astype(BF16), preferred_element_type=F32)
        acc_ref[h] = alpha * acc_ref[h] + pv
    m_ref[...] = m_new

    @pl.when(j == pl.num_programs(1) - 1)
    def _finish():
        s_cur = jnp.sum(qbd_ref[...] * kc_ref[0], axis=-1, keepdims=True)
        m_old = m_ref[...]
        m_fin = jnp.maximum(m_old, s_cur)
        a_fin = jnp.exp2(m_old - m_fin)
        p_cur = jnp.exp2(s_cur - m_fin)
        l_fin = a_fin * l_ref[...] + p_cur
        lam = lam_ref[0]
        sw = sw_ref[...]
        vc = vc_ref[0]
        for h in range(A_HEADS):
            cols = slice(h * A_VDIM, (h + 1) * A_VDIM)
            a = (a_fin * acc_ref[h] + p_cur * vc[:, cols]) / l_fin
            o = a[2 * h:2 * h + 1, :] - lam * a[2 * h + 1:2 * h + 2, :]
            o_ref[0, :, cols] = _rms(o, sw) * lam_scale


def _decode_attn(page_table, lam, qd, kd, vd, sw, cache_kt, cache_vf, e, lam_scale,
                 pages):
    nb, n_pages = page_table.shape
    kern = functools.partial(_decode_attn_kernel, pages=pages, lam_scale=lam_scale)
    tok = lambda b, j, pt: (b, 0, 0)

    def page_spec(p):
        return pl.BlockSpec((None, None, A_QK, PAGE_SIZE),
                            lambda b, j, pt: (e, pt[b, j * pages + p], 0, 0))

    grid_spec = pltpu.PrefetchScalarGridSpec(
        num_scalar_prefetch=1,
        grid=(nb, n_pages // pages),
        in_specs=([pl.BlockSpec(memory_space=pltpu.SMEM),
                   pl.BlockSpec((1, 1, A_QK), tok),
                   pl.BlockSpec((1, 1, A_QK), tok),
                   pl.BlockSpec((1, 1, A_V), tok),
                   pl.BlockSpec((1, LANES), lambda b, j, pt: (0, 0))]
                  + [page_spec(p) for p in range(pages)] * 2),
        out_specs=pl.BlockSpec((1, 1, A_V), tok),
        scratch_shapes=[pltpu.VMEM((DEC_ROWS, A_QK), F32),
                        pltpu.VMEM((DEC_ROWS, LANES), F32),
                        pltpu.VMEM((DEC_ROWS, LANES), F32),
                        pltpu.VMEM((A_HEADS, DEC_ROWS, A_VDIM), F32)])
    return pl.pallas_call(
        kern,
        grid_spec=grid_spec,
        out_shape=jax.ShapeDtypeStruct((nb, 1, A_V), F32),
        compiler_params=_cparams(("parallel", "arbitrary")),
        name="decode_diff_attn",
    )(page_table, lam, qd, kd, vd, sw, *([cache_kt] * pages), *([cache_vf] * pages))


def _ffn_rows(x, nw_in_ref, wi_ref, wo_ref, nw_out_ref):
    xn = _rms(x, nw_in_ref[...]).astype(BF16)
    y = jnp.zeros(x.shape, F32)
    for c in range(D_FF // FFN_CHUNK):
        lo = c * FFN_CHUNK
        gate = jnp.dot(xn, wi_ref[:, lo:lo + FFN_CHUNK], preferred_element_type=F32)
        up = jnp.dot(xn, wi_ref[:, D_FF + lo:D_FF + lo + FFN_CHUNK],
                     preferred_element_type=F32)
        act = (gate * _sigmoid(gate) * up).astype(BF16)
        y = y + jnp.dot(act, wo_ref[lo:lo + FFN_CHUNK, :], preferred_element_type=F32)
    return x + _rms(y, nw_out_ref[...])


def _even_tail_kernel(*refs, tm, seq_tiles, explicit_prev):
    if explicit_prev:
        o_ref, bg_ref, u_ref, um1_ref, um2_ref = refs[:5]
        rest = refs[5:]
        u = u_ref[...]
        um1 = um1_ref[...]
        um2 = um2_ref[...]
    else:
        o_ref, bg_ref, u_ref, halo_ref = refs[:4]
        rest = refs[4:]
        u = u_ref[...]
        halo = halo_ref[...]
        halo = jnp.where(pl.program_id(0) % seq_tiles == 0, 0.0, halo)
        row = lax.broadcasted_iota(jnp.int32, u.shape, 0)
        h1 = halo[SUBLANES - 1:SUBLANES, :]
        h2 = halo[SUBLANES - 2:SUBLANES - 1, :]
        um1 = jnp.where(row == 0, h1, pltpu.roll(u, 1, 0))
        um2 = jnp.where(row == 0, h2, jnp.where(row == 1, h1, pltpu.roll(u, 2, 0)))
    cw_ref, w_ref, x_ref, nw_ref, nw_in_ref, wi_ref, wo_ref, nw_out_ref, out_ref = rest
    cw = cw_ref[...]
    conv = um2 * cw[0:1, :] + um1 * cw[1:2, :] + u * cw[2:3, :]
    yb = bg_ref[...] * conv
    y = (jnp.dot(o_ref[...].astype(BF16), w_ref[0:A_V, :], preferred_element_type=F32)
         + jnp.dot(yb.astype(BF16), w_ref[A_V:A_V + B_WIDTH, :],
                   preferred_element_type=F32))
    x_mid = x_ref[...] + _rms(y, nw_ref[...])
    out_ref[...] = _ffn_rows(x_mid, nw_in_ref, wi_ref, wo_ref, nw_out_ref)


def _even_tail(o, bg, u, prev, cw, w, li, x, nw, wi, wo, layer, tm, seq_len):
    n = x.shape[0]
    row = lambda i: (i, 0)
    half = pl.BlockSpec((tm, B_WIDTH), row)
    vec = _const_spec((1, D_MODEL))
    explicit_prev = isinstance(prev, tuple)
    if explicit_prev:
        prev_args = list(prev)
        prev_specs = [half, half]
    else:
        prev_args = [u]
        prev_specs = [pl.BlockSpec(
            (SUBLANES, B_WIDTH),
            lambda i: (jnp.maximum(i * (tm // SUBLANES) - 1, 0), 0))]
    kern = functools.partial(_even_tail_kernel, tm=tm, seq_tiles=seq_len // tm,
                             explicit_prev=explicit_prev)
    return pl.pallas_call(
        kern,
        grid=(n // tm,),
        in_specs=([half, half, half] + prev_specs
                  + [_const_spec((CONV_W, B_WIDTH)),
                     _layer_spec(w, li),
                     pl.BlockSpec((tm, D_MODEL), row),
                     vec, vec, _layer_spec(wi, layer), _layer_spec(wo, layer), vec]),
        out_specs=pl.BlockSpec((tm, D_MODEL), row),
        out_shape=jax.ShapeDtypeStruct((n, D_MODEL), F32),
        compiler_params=_cparams(("parallel",)),
        name="even_tail",
    )(o, bg, u, *prev_args, cw, w, x, nw[1], nw[2], wi, wo, nw[3])


def _odd_in_kernel(x_ref, nw_ref, w_ref, lb_ref, z_ref):
    xn = _rms(x_ref[...], nw_ref[...]).astype(BF16)

    def proj(c):
        return jnp.dot(xn, w_ref[:, c * D_MODEL:(c + 1) * D_MODEL],
                       preferred_element_type=F32)

    zq = proj(0)
    z_ref[:, 0:HG_FDIM] = zq * _sigmoid(zq)
    lb = lb_ref[...]
    z_ref[:, HG_FDIM:2 * HG_FDIM] = lb + (1.0 - lb) * _sigmoid(proj(1))
    z_ref[:, 2 * HG_FDIM:2 * HG_FDIM + D_MODEL] = proj(2)
    zg = proj(3)
    z_ref[:, 2 * HG_FDIM + D_MODEL:ODD_IN] = zg * _sigmoid(zg)


def _odd_in(x, nw, w, li, lb, tm):
    n = x.shape[0]
    row = lambda i: (i, 0)
    return pl.pallas_call(
        _odd_in_kernel,
        grid=(n // tm,),
        in_specs=[pl.BlockSpec((tm, D_MODEL), row),
                  _const_spec((1, D_MODEL)),
                  _layer_spec(w, li),
                  _const_spec((1, HG_FDIM))],
        out_specs=pl.BlockSpec((tm, ODD_IN), row),
        out_shape=jax.ShapeDtypeStruct((n, ODD_IN), F32),
        compiler_params=_cparams(("parallel",)),
        name="odd_in",
    )(x, nw, w, lb)


def _gla_tables():
    c = GLA_CHUNK
    t = np.arange(c)[:, None]
    r = np.arange(c)[None, :]
    mats = []
    for lev in range(GLA_SUM_LEVELS):
        half = 1 << lev
        mid = (t // (2 * half)) * (2 * half) + half
        mats.append(np.where(t >= mid, (r >= mid) & (r <= t), (r > t) & (r < mid)))
    mats.append(r <= t)
    sums = np.concatenate(mats, axis=0).astype(np.float32)
    high_bit = np.floor(np.log2(np.maximum(t ^ r, 1))).astype(np.int32)
    level = np.where(t > r, high_bit, np.where(t == r, GLA_LEVELS, GLA_LEVELS + 1))
    sign = np.stack([np.where((t >> lev) & 1 == 1, 1.0, -1.0) * np.ones((1, HG_DK))
                     for lev in range(GLA_SUM_LEVELS, GLA_LEVELS)]).astype(np.float32)
    return (jnp.asarray(sums, BF16), jnp.asarray(level.astype(np.int32)),
            jnp.asarray(sign))


def _gla_kernel(q_ref, f_ref, v_ref, g_ref, gw_ref, sums_ref, level_ref, sign_ref,
                og_ref, sout_ref, st_ref, o_s, *, rows):
    c = GLA_CHUNK
    tb = pl.program_id(2)

    @pl.when(tb == 0)
    def _zero_state():
        st_ref[...] = jnp.zeros(st_ref.shape, F32)

    q_all = q_ref[...]
    fg = f_ref[...]
    logf_all = jnp.log(fg) * LOG2E
    k_all = 1.0 - fg
    v_all = v_ref[...]
    level = level_ref[...]
    row_id = lax.broadcasted_iota(jnp.int32, (c, HG_DK), 0)
    contract_last = (((1,), (1,)), ((), ()))

    st = st_ref[...]
    for ch in range(rows // c):
        sl = slice(ch * c, (ch + 1) * c)
        q = q_all[sl]
        k = k_all[sl]
        v = v_all[sl].astype(BF16)
        logf = logf_all[sl]
        hi = logf.astype(BF16)
        lo = (logf - hi.astype(F32)).astype(BF16)
        w = jnp.dot(sums_ref[...], jnp.concatenate([hi, lo], axis=1),
                    preferred_element_type=F32)
        w = w[:, 0:HG_DK] + w[:, HG_DK:2 * HG_DK]
        cum = w[GLA_SUM_LEVELS * c:(GLA_SUM_LEVELS + 1) * c]
        a = jnp.zeros((c, c), F32)
        for lev in range(GLA_LEVELS):
            if lev < GLA_SUM_LEVELS:
                expo = w[lev * c:(lev + 1) * c]
            else:
                half = 1 << lev
                at_split = jnp.concatenate(
                    [jnp.broadcast_to(cum[b0 + half - 1:b0 + half, :], (2 * half, HG_DK))
                     for b0 in range(0, c, 2 * half)], axis=0)
                expo = (cum - at_split) * sign_ref[lev - GLA_SUM_LEVELS]
            upper = ((row_id >> lev) & 1) == 1
            y = (jnp.where(upper, q, k) * jnp.exp2(expo)).astype(BF16)
            a = jnp.where(level == lev,
                          lax.dot_general(y, y, contract_last,
                                          preferred_element_type=F32), a)
        a = jnp.where(level == GLA_LEVELS,
                      lax.dot_general(q.astype(BF16), k.astype(BF16), contract_last,
                                      preferred_element_type=F32), a)
        o_intra = jnp.dot(a.astype(BF16), v, preferred_element_type=F32)
        total = cum[c - 1:c, :]
        qh = (q * jnp.exp2(cum)).astype(BF16)
        kh = (k * jnp.exp2(total - cum)).astype(BF16)
        o_inter = lax.dot_general(qh, st.astype(BF16), contract_last,
                                  preferred_element_type=F32)
        st = st * jnp.exp2(total) + lax.dot_general(v, kh, (((0,), (0,)), ((), ())),
                                                    preferred_element_type=F32)
        o_s[sl, :] = o_inter + o_intra
    st_ref[...] = st
    og_ref[...] = _rms(o_s[...], gw_ref[...]) * g_ref[...]

    @pl.when(tb == pl.num_programs(2) - 1)
    def _store_state():
        sout_ref[0, 0] = st.T


def _gla(z, gw, nseq, t, rows):
    nt = t // rows
    sums, level, sign = _gla_tables()

    def col_spec(base):
        return pl.BlockSpec((rows, HG_DK),
                            lambda s, h, tb: (s * nt + tb, base + h))

    kern = functools.partial(_gla_kernel, rows=rows)
    return pl.pallas_call(
        kern,
        grid=(nseq, HG_HEADS, nt),
        in_specs=[col_spec(0), col_spec(HG_HEADS), col_spec(2 * HG_HEADS),
                  col_spec(3 * HG_HEADS),
                  pl.BlockSpec((1, HG_DV), lambda s, h, tb: (0, 0)),
                  _const_spec(sums.shape), _const_spec(level.shape),
                  _const_spec(sign.shape)],
        out_specs=[pl.BlockSpec((rows, HG_DV), lambda s, h, tb: (s * nt + tb, h)),
                   pl.BlockSpec((1, 1, HG_DK, HG_DV), lambda s, h, tb: (s, h, 0, 0))],
        out_shape=[jax.ShapeDtypeStruct((nseq * t, D_MODEL), F32),
                   jax.ShapeDtypeStruct((nseq, HG_HEADS, HG_DK, HG_DV), F32)],
        scratch_shapes=[pltpu.VMEM((HG_DV, HG_DK), F32),
                        pltpu.VMEM((rows, HG_DV), F32)],
        compiler_params=_cparams(("parallel", "parallel", "arbitrary")),
        name="hgrn2_gla",
    )(z, z, z, z, gw, sums, level, sign)


def _gla_decode_kernel(z_ref, gw_ref, s0_ref, og_ref, sout_ref):
    z = z_ref[0]
    eye = (lax.broadcasted_iota(jnp.int32, (HG_DK, HG_DK), 0)
           == lax.broadcasted_iota(jnp.int32, (HG_DK, HG_DK), 1))

    def column(row_vec):
        return jnp.sum(jnp.where(eye, row_vec, 0.0), axis=-1, keepdims=True)

    gw = gw_ref[...]
    for h in range(HG_HEADS):
        cols = slice(h * HG_DK, (h + 1) * HG_DK)
        q = z[:, cols]
        fg = z[:, HG_FDIM + h * HG_DK:HG_FDIM + (h + 1) * HG_DK]
        v = z[:, 2 * HG_FDIM + h * HG_DV:2 * HG_FDIM + (h + 1) * HG_DV]
        gate = z[:, 2 * HG_FDIM + D_MODEL + h * HG_DV:2 * HG_FDIM + D_MODEL + (h + 1) * HG_DV]
        s_new = column(fg) * s0_ref[h] + column(1.0 - fg) * v
        sout_ref[0, h] = s_new
        o = jnp.sum(column(q) * s_new, axis=0, keepdims=True)
        og_ref[0, :, cols] = _rms(o, gw) * gate


def _gla_decode(z, gw, states, li):
    nseq = z.shape[0]
    tok = lambda s: (s, 0, 0)
    heads = (HG_HEADS, HG_DK, HG_DV)
    return pl.pallas_call(
        _gla_decode_kernel,
        grid=(nseq,),
        in_specs=[pl.BlockSpec((1, 1, ODD_IN), tok),
                  _const_spec((1, HG_DV)),
                  pl.BlockSpec((None, None) + heads, lambda s: (li, s, 0, 0, 0))],
        out_specs=[pl.BlockSpec((1, 1, D_MODEL), tok),
                   pl.BlockSpec((1,) + heads, lambda s: (s, 0, 0, 0))],
        out_shape=[jax.ShapeDtypeStruct((nseq, 1, D_MODEL), F32),
                   jax.ShapeDtypeStruct((nseq,) + heads, F32)],
        compiler_params=_cparams(("parallel",)),
        name="hgrn2_decode",
    )(z.reshape(nseq, 1, ODD_IN), gw, states)


def _odd_tail_kernel(a_ref, w_ref, x_ref, nw_ref, nw_in_ref, wi_ref, wo_ref, nw_out_ref,
                     out_ref):
    y = jnp.dot(a_ref[...].astype(BF16), w_ref[...], preferred_element_type=F32)
    x_mid = x_ref[...] + _rms(y, nw_ref[...])
    out_ref[...] = _ffn_rows(x_mid, nw_in_ref, wi_ref, wo_ref, nw_out_ref)


def _odd_tail(a, w, li, x, nw, wi, wo, layer, tm):
    n = x.shape[0]
    row = lambda i: (i, 0)
    vec = _const_spec((1, D_MODEL))
    return pl.pallas_call(
        _odd_tail_kernel,
        grid=(n // tm,),
        in_specs=[pl.BlockSpec((tm, D_MODEL), row),
                  _layer_spec(w, li),
                  pl.BlockSpec((tm, D_MODEL), row),
                  vec, vec, _layer_spec(wi, layer), _layer_spec(wo, layer), vec],
        out_specs=pl.BlockSpec((tm, D_MODEL), row),
        out_shape=jax.ShapeDtypeStruct((n, D_MODEL), F32),
        compiler_params=_cparams(("parallel",)),
        name="odd_tail",
    )(a, w, x, nw[1], nw[2], wi, wo, nw[3])


def _rope_tables(pos):
    half = HEAD_DIM // 2
    inv = ROPE_THETA ** (-jnp.arange(half, dtype=F32) / half)
    ang = pos.astype(F32)[:, None] * inv[None, :]
    cos = jnp.tile(jnp.cos(ang), (1, LANES // half))
    sin = jnp.tile(jnp.concatenate([-jnp.sin(ang), jnp.sin(ang)], axis=1),
                   (1, LANES // HEAD_DIM))
    return cos, sin


def kernel(x_prompt, x_sample, cache_k, cache_v, state_conv, state_hgrn, page_table,
           norm_w, w_in_even, w_out_even, conv_w, lambda_q1, lambda_k1, lambda_q2,
           lambda_k2, subln_w, w_in_odd, w_out_odd, hgrn_lb, gnorm_w, w_ffn_in,
           w_ffn_out):
    nb, t, _ = x_prompt.shape
    nd, td, _ = x_sample.shape
    assert td == 1
    depth = norm_w.shape[0]
    n_pages = page_table.shape[1]
    past_len = n_pages * PAGE_SIZE
    tm_p = 512
    tm_d = nd
    dec_pages = 32
    gla_rows = 4096
    flash_tq, flash_tk = 2048, 1024

    cos_p, sin_p = _rope_tables(jnp.arange(t))
    cos_d, sin_d = _rope_tables(jnp.full((nd,), past_len))
    p_lb = jax.nn.softmax(hgrn_lb.astype(F32), axis=0)
    lower_bounds = jnp.cumsum(p_lb, axis=0) - p_lb[0]

    n_even, n_phys = cache_k.shape[:2]
    ck = jnp.transpose(cache_k, (0, 1, 3, 4, 5, 2)).reshape(n_even, n_phys, A_QK, PAGE_SIZE)
    cv = cache_v.reshape(n_even, n_phys, PAGE_SIZE * A_HEADS, A_VDIM)
    nw = norm_w.reshape(depth, 4, 1, D_MODEL)
    w_in_e, w_out_e = w_in_even.astype(BF16), w_out_even.astype(BF16)
    w_in_o, w_out_o = w_in_odd.astype(BF16), w_out_odd.astype(BF16)
    w_ffn_i, w_ffn_o = w_ffn_in.astype(BF16), w_ffn_out.astype(BF16)

    xp = x_prompt.reshape(nb * t, D_MODEL)
    xs = x_sample.reshape(nd, D_MODEL)
    kp_l, vp_l, cp_l, hp_l = [], [], [], []
    ks_l, vs_l, cs_l, hs_l = [], [], [], []
    for layer in range(depth):
        if layer % 2 == 0:
            e = layer // 2
            lam_init = 0.8 - 0.6 * math.exp(-0.3 * layer)
            lam = (jnp.exp(jnp.sum(lambda_q1[e].astype(F32) * lambda_k1[e].astype(F32)))
                   - jnp.exp(jnp.sum(lambda_q2[e].astype(F32) * lambda_k2[e].astype(F32)))
                   + lam_init).reshape(1)
            sw = subln_w[e].reshape(1, A_VDIM)
            cw = conv_w[e]

            q, k, v, bg, u, v_rows = _even_in(xp, nw[layer, 0], w_in_e, e, cos_p, sin_p, tm_p)
            o = _flash(lam, q, k, v, sw, nb, t, 1.0 - lam_init, flash_tq, flash_tk)
            xp = _even_tail(o, bg, u, None, cw, w_out_e, e, xp, nw[layer], w_ffn_i, w_ffn_o,
                            layer, tm_p, t)

            qd, kd, vd, bgd, ud, _ = _even_in(xs, nw[layer, 0], w_in_e, e, cos_d, sin_d, tm_d)
            od = _decode_attn(page_table, lam, qd.reshape(nd, 1, A_QK),
                              kd.reshape(nd, 1, A_QK), vd.reshape(nd, 1, A_V), sw,
                              ck, cv, e, 1.0 - lam_init, dec_pages)
            sc = state_conv[e]
            xs = _even_tail(od.reshape(nd, A_V), bgd, ud, (sc[:, 1], sc[:, 0]), cw,
                            w_out_e, e, xs, nw[layer], w_ffn_i, w_ffn_o, layer, tm_d, td)

            kp_l.append(k.reshape(nb, t, A_HEADS, 2, HEAD_DIM))
            vp_l.append(v_rows.reshape(nb, t, A_HEADS, A_VDIM))
            cp_l.append(u.reshape(nb, t, B_WIDTH)[:, t - (CONV_W - 1):])
            ks_l.append(kd.reshape(nd, td, A_HEADS, 2, HEAD_DIM))
            vs_l.append(vd.reshape(nd, td, A_HEADS, A_VDIM))
            cs_l.append(jnp.concatenate([sc[:, 1:], ud[:, None, :]], axis=1))
        else:
            o_idx = layer // 2
            lb = lower_bounds[layer].reshape(1, HG_FDIM)
            gw = gnorm_w[o_idx].reshape(1, HG_DV)

            z = _odd_in(xp, nw[layer, 0], w_in_o, o_idx, lb, tm_p)
            og, sp = _gla(z, gw, nb, t, gla_rows)
            xp = _odd_tail(og, w_out_o, o_idx, xp, nw[layer], w_ffn_i, w_ffn_o, layer, tm_p)

            zd = _odd_in(xs, nw[layer, 0], w_in_o, o_idx, lb, tm_d)
            ogd, ss = _gla_decode(zd, gw, state_hgrn, o_idx)
            xs = _odd_tail(ogd.reshape(nd, D_MODEL), w_out_o, o_idx, xs, nw[layer], w_ffn_i,
                           w_ffn_o, layer, tm_d)
            hp_l.append(sp)
            hs_l.append(ss)

    return (xp.reshape(nb, t, D_MODEL), xs.reshape(nd, td, D_MODEL),
            jnp.stack(kp_l), jnp.stack(vp_l), jnp.stack(cp_l), jnp.stack(hp_l),
            jnp.stack(ks_l), jnp.stack(vs_l), jnp.stack(cs_l), jnp.stack(hs_l))
```
